```python
import jax, jax.numpy as jnp
from jax import lax
import numpy as np

D_MODEL = 1024
BATCH = 8
SEQ = 2048
DEPTH = 2

GRID_W = 64
CTX_LEN = 256
HEAD_DIM = 64
N_HEADS_TOTAL = D_MODEL // HEAD_DIM
RET_HEADS = N_HEADS_TOTAL // 4
MLSTM_HEADS = N_HEADS_TOTAL // 4
NA_HEADS = N_HEADS_TOTAL - RET_HEADS - MLSTM_HEADS
RET_W = RET_HEADS * HEAD_DIM
MLSTM_W = MLSTM_HEADS * HEAD_DIM
NA_W = NA_HEADS * HEAD_DIM
D_MIX = RET_W + MLSTM_W + NA_W
CHUNK = 128
MLSTM_CONV = 3
NA_WIN_R = 8
NA_WIN_C = 16
ROPE_BASE = 10000.0
D_FF = 2816
N_EXPERTS = 8
TOP_K = 2
N_DENSE = (DEPTH + 1) // 2
N_MOE = DEPTH // 2
EPS = 1e-6
IN_SIZES = (RET_W,) * 4 + (MLSTM_W,) * 4 + (4 * MLSTM_HEADS,) + (NA_W,) * 3
IN_SPLITS = tuple(int(s) for s in np.cumsum(IN_SIZES)[:-1])
D_IN = int(sum(IN_SIZES))

kernel_name = "hybrid_ret_mlstm_natten_moe_dit"

F32 = jnp.float32


def _identity(a):
    return a


def _flip(a):
    return a[:, ::-1]


def rmsnorm(x, g):
    x32 = x.astype(F32)
    y = x32 * lax.rsqrt(jnp.mean(x32 * x32, axis=-1, keepdims=True) + EPS)
    return (y * g.astype(F32)).astype(x.dtype)


def head_rms(y):
    y32 = y.astype(F32)
    return y32 * lax.rsqrt(jnp.mean(y32 * y32, axis=-1, keepdims=True) + EPS)


def qk_norm(y, g):
    return (head_rms(y) * g.astype(F32)).astype(y.dtype)


def _rotate(x, ang):
    x1, x2 = jnp.split(x, 2, axis=-1)
    cos = jnp.cos(ang)[None, :, None, :].astype(x.dtype)
    sin = jnp.sin(ang)[None, :, None, :].astype(x.dtype)
    return jnp.concatenate([x1 * cos - x2 * sin, x1 * sin + x2 * cos], axis=-1)


def rope_2d(x):
    T = x.shape[1]
    t = jnp.arange(T)
    row = (t // GRID_W).astype(F32)
    col = (t % GRID_W).astype(F32)
    half = HEAD_DIM // 2
    inv = ROPE_BASE ** (-jnp.arange(0, half, 2, dtype=F32) / half)
    xr = _rotate(x[..., :half], row[:, None] * inv[None, :])
    xc = _rotate(x[..., half:], col[:, None] * inv[None, :])
    return jnp.concatenate([xr, xc], axis=-1)


def dwconv_centered(x, w, b):
    K = w.shape[0]
    y = lax.conv_general_dilated(x, w[:, None, :].astype(x.dtype), window_strides=(1,),
                                 padding=[((K - 1) // 2, K // 2)],
                                 dimension_numbers=('NWC', 'WIO', 'NWC'),
                                 feature_group_count=x.shape[-1])
    return y + b.astype(x.dtype)


def _chunks(a):
    B, T, H, d = a.shape
    return a.reshape(B, T // CHUNK, CHUNK, H, d).transpose(1, 0, 3, 2, 4).astype(F32)


def _unchunks(a):
    n, B, H, C, d = a.shape
    return a.transpose(1, 0, 3, 2, 4).reshape(B, n * C, H, d)


def _gate_chunks(a):
    B, T, H = a.shape
    return a.reshape(B, T // CHUNK, CHUNK, H).transpose(1, 0, 3, 2).astype(F32)


def retention_scan(q, k, v, log_gamma, s0, with_output):
    d = q.shape[-1]
    qc, kc, vc = _chunks(q), _chunks(k * (d ** -0.5)), _chunks(v)
    lg = log_gamma.astype(F32)
    idx = jnp.arange(CHUNK, dtype=F32)
    diff = idx[:, None] - idx[None, :]
    intra = jnp.where(diff >= 0, jnp.exp(lg[:, None, None] * jnp.maximum(diff, 0.0)), 0.0)
    q_dec = jnp.exp(lg[:, None] * (idx[None, :] + 1.0))
    k_dec = jnp.exp(lg[:, None] * (CHUNK - 1.0 - idx[None, :]))
    c_dec = jnp.exp(lg * CHUNK)

    def step(s, inp):
        qj, kj, vj = inp
        s_new = c_dec[None, :, None, None] * s + jnp.einsum(
            'bhcd,bhce->bhde', kj * k_dec[None, :, :, None], vj)
        if not with_output:
            return s_new, None
        att = jnp.einsum('bhid,bhjd->bhij', qj, kj) * intra[None]
        o = jnp.einsum('bhij,bhje->bhie', att, vj) + jnp.einsum(
            'bhid,bhde->bhie', qj * q_dec[None, :, :, None], s)
        return s_new, o

    s_final, o = lax.scan(step, s0.astype(F32), (qc, kc, vc))
    return (_unchunks(o) if with_output else None), s_final


def mlstm_scan(q, k, v, i_pre, logf, state, with_output):
    d = q.shape[-1]
    qc, kc, vc = _chunks(q), _chunks(k * (d ** -0.5)), _chunks(v)
    ic, fc = _gate_chunks(i_pre), _gate_chunks(logf)
    causal = jnp.tril(jnp.ones((CHUNK, CHUNK), dtype=bool))

    def step(carry, inp):
        C_, n_, m_ = carry
        qj, kj, vj, ij, fj = inp
        b = jnp.cumsum(fj, axis=-1)
        b_last = b[..., -1]
        w_state = b_last[..., None] - b + ij
        m_new = jnp.maximum(b_last + m_, jnp.max(w_state, axis=-1))
        decay_prev = jnp.exp(b_last + m_ - m_new)
        wk = jnp.exp(w_state - m_new[..., None])
        C_new = decay_prev[..., None, None] * C_ + jnp.einsum('bhc,bhcd,bhce->bhde', wk, kj, vj)
        n_new = decay_prev[..., None] * n_ + jnp.einsum('bhc,bhcd->bhd', wk, kj)
        if not with_output:
            return (C_new, n_new, m_new), None
        Dm = jnp.where(causal, b[..., :, None] - b[..., None, :] + ij[..., None, :], -jnp.inf)
        inter = b + m_[..., None]
        m_row = jnp.maximum(jnp.max(Dm, axis=-1), inter)
        s = jnp.einsum('bhid,bhjd->bhij', qj, kj) * jnp.exp(Dm - m_row[..., None])
        g = jnp.exp(inter - m_row)
        num = jnp.einsum('bhij,bhje->bhie', s, vj) + g[..., None] * jnp.einsum('bhid,bhde->bhie', qj, C_)
        den = jnp.sum(s, axis=-1) + g * jnp.einsum('bhid,bhd->bhi', qj, n_)
        h = num / jnp.maximum(jnp.abs(den), jnp.exp(-m_row))[..., None]
        return (C_new, n_new, m_new), h

    st, h = lax.scan(step, state, (qc, kc, vc, ic, fc))
    return (_unchunks(h) if with_output else None), st


def bidir_retention(q, k, v, q_c, k_c, v_c, log_gamma, with_ctx_out):
    B, _, H, d = q.shape
    s0 = jnp.zeros((B, H, d, d), F32)
    y, y_c = 0.0, 0.0
    for dn in range(2):
        f = _identity if dn == 0 else _flip
        oc, sc = retention_scan(f(q_c), f(k_c), f(v_c), log_gamma[dn], s0, with_ctx_out)
        ox, _ = retention_scan(f(q), f(k), f(v), log_gamma[dn], sc, True)
        y = y + f(ox)
        if with_ctx_out:
            y_c = y_c + f(oc)
    return y, (y_c if with_ctx_out else None)


def bidir_mlstm(q, k, v, g, q_c, k_c, v_c, g_c, with_ctx_out):
    B, _, H, d = q.shape
    init = (jnp.zeros((B, H, d, d), F32), jnp.zeros((B, H, d), F32), jnp.zeros((B, H), F32))
    y, y_c = 0.0, 0.0
    for dn in range(2):
        f = _identity if dn == 0 else _flip
        oc, st = mlstm_scan(f(q_c), f(k_c), f(v_c), f(g_c[:, :, dn]),
                            f(jax.nn.log_sigmoid(g_c[:, :, 2 + dn])), init, with_ctx_out)
        ox, _ = mlstm_scan(f(q), f(k), f(v), f(g[:, :, dn]),
                           f(jax.nn.log_sigmoid(g[:, :, 2 + dn])), st, True)
        y = y + f(ox)
        if with_ctx_out:
            y_c = y_c + f(oc)
    return y, (y_c if with_ctx_out else None)


def neighbourhood_attention(q, k, v, k_ctx, v_ctx, rpb):
    B, T, H, d = q.shape
    rows = T // GRID_W
    kr = min(NA_WIN_R, rows)
    r = jnp.arange(rows)
    key_rows = jnp.clip(r - kr // 2, 0, rows - kr)[:, None] + jnp.arange(kr)[None, :]
    cq = jnp.arange(GRID_W)
    start_c = jnp.clip(cq - NA_WIN_C // 2, 0, GRID_W - NA_WIN_C)
    col_ok = (cq[None, :] >= start_c[:, None]) & (cq[None, :] < start_c[:, None] + NA_WIN_C)
    dr = (key_rows - r[:, None] + NA_WIN_R - 1)[:, None, :, None]
    dc = jnp.clip(cq[None, :] - cq[:, None] + NA_WIN_C - 1, 0, 2 * NA_WIN_C - 2)[None, :, None, :]
    bias = rpb.astype(F32)[:, dr, dc]
    scale = HEAD_DIM ** -0.5
    qg = q.reshape(B, rows, GRID_W, H, d)
    kg = k.reshape(B, rows, GRID_W, H, d)[:, key_rows]
    vg = v.reshape(B, rows, GRID_W, H, d)[:, key_rows]
    s_loc = jnp.einsum('brqhd,brkwhd->bhrqkw', qg, kg).astype(F32) * scale + bias[None]
    s_loc = jnp.where(col_ok[None, None, None, :, None, :], s_loc, -jnp.inf)
    s_ctx = jnp.einsum('brqhd,bchd->bhrqc', qg, k_ctx).astype(F32) * scale
    n_loc = kr * GRID_W
    p = jax.nn.softmax(jnp.concatenate([s_loc.reshape(B, H, rows, GRID_W, n_loc), s_ctx], axis=-1), axis=-1)
    p_loc = p[..., :n_loc].reshape(B, H, rows, GRID_W, kr, GRID_W)
    o = jnp.einsum('bhrqkw,brkwhd->brqhd', p_loc, vg) + jnp.einsum('bhrqc,bchd->brqhd', p[..., n_loc:], v_ctx)
    return o.reshape(B, T, H, d).astype(q.dtype)


def context_attention(q, k, v):
    s = jnp.einsum('bqhd,bkhd->bhqk', q, k).astype(F32) * (HEAD_DIM ** -0.5)
    p = jax.nn.softmax(s, axis=-1)
    return jnp.einsum('bhqk,bkhd->bqhd', p, v).astype(q.dtype)


def _prep(h, w_in, conv_w, conv_b, gate_b, latent):
    B, T, _ = h.shape
    rq, rk, rv, rg, mq, mk, mv, mo, mg, nq, nk, nv = jnp.split(h @ w_in, IN_SPLITS, axis=-1)
    rq = rq.reshape(B, T, RET_HEADS, HEAD_DIM)
    rk = rk.reshape(B, T, RET_HEADS, HEAD_DIM)
    if latent:
        rq, rk = rope_2d(rq), rope_2d(rk)
    rv = rv.reshape(B, T, RET_HEADS, HEAD_DIM)
    mqk = jax.nn.silu(dwconv_centered(jnp.concatenate([mq, mk], axis=-1), conv_w, conv_b))
    mq, mk = jnp.split(mqk, 2, axis=-1)
    mq = mq.reshape(B, T, MLSTM_HEADS, HEAD_DIM)
    mk = mk.reshape(B, T, MLSTM_HEADS, HEAD_DIM)
    mv = mv.reshape(B, T, MLSTM_HEADS, HEAD_DIM)
    mg = mg.reshape(B, T, 4, MLSTM_HEADS).astype(F32) + gate_b.astype(F32)
    nq = nq.reshape(B, T, NA_HEADS, HEAD_DIM)
    nk = nk.reshape(B, T, NA_HEADS, HEAD_DIM)
    nv = nv.reshape(B, T, NA_HEADS, HEAD_DIM)
    return rq, rk, rv, rg, mq, mk, mv, mo, mg, nq, nk, nv


def _merge(rg, ret, mo, ml, na, w_out):
    B, T = rg.shape[:2]
    ret = (jax.nn.silu(rg.astype(F32)) * head_rms(ret).reshape(B, T, RET_W)).astype(rg.dtype)
    ml = (jax.nn.sigmoid(mo.astype(F32)) * head_rms(ml).reshape(B, T, MLSTM_W)).astype(rg.dtype)
    return jnp.concatenate([ret, ml, na.reshape(B, T, NA_W)], axis=-1) @ w_out


def mixer(hx, hc, w_in, ret_decay, conv_w, conv_b, gate_b, q_gain, k_gain, rpb, w_out, with_ctx_out):
    X = _prep(hx, w_in, conv_w, conv_b, gate_b, True)
    C = _prep(hc, w_in, conv_w, conv_b, gate_b, False)
    log_gamma = jax.nn.log_sigmoid(ret_decay.astype(F32))
    ret_x, ret_c = bidir_retention(X[0], X[1], X[2], C[0], C[1], C[2], log_gamma, with_ctx_out)
    ml_x, ml_c = bidir_mlstm(X[4], X[5], X[6], X[8], C[4], C[5], C[6], C[8], with_ctx_out)
    nk_c = qk_norm(C[10], k_gain)
    na_x = neighbourhood_attention(qk_norm(X[9], q_gain), qk_norm(X[10], k_gain), X[11], nk_c, C[11], rpb)
    out_x = _merge(X[3], ret_x, X[7], ml_x, na_x, w_out)
    if not with_ctx_out:
        return out_x, None
    na_c = context_attention(qk_norm(C[9], q_gain), nk_c, C[11])
    return out_x, _merge(C[3], ret_c, C[7], ml_c, na_c, w_out)


def swiglu(h, wg, wu, wd):
    return (jax.nn.silu(h @ wg) * (h @ wu)) @ wd


def moe_swiglu(h, w_router, w_gate, w_up, w_down):
    B, T, D = h.shape
    t = h.reshape(B * T, D)
    logits = (t @ w_router).astype(F32)
    top_v, top_i = lax.top_k(logits, TOP_K)
    wts = jax.nn.softmax(top_v, axis=-1)
    combine = jnp.sum(jax.nn.one_hot(top_i, N_EXPERTS, dtype=F32) * wts[..., None], axis=1)
    out = jnp.zeros_like(t)
    for e in range(N_EXPERTS):
        out = out + combine[:, e:e + 1].astype(t.dtype) * swiglu(t, w_gate[e], w_up[e], w_down[e])
    return out.reshape(B, T, D)


def setup_inputs(seed: int = 0) -> dict:
    key = jax.random.key(seed)
    ks = jax.random.split(key, 26)
    D = D_MODEL

    def nrm(k, shape, s):
        return jax.random.normal(k, shape, F32) * s

    gam = 1.0 - 2.0 ** (-5.0 - jnp.arange(RET_HEADS, dtype=F32))
    i_b = nrm(ks[12], (DEPTH, 2, MLSTM_HEADS), 0.1)
    f_b = jnp.linspace(3.0, 6.0, MLSTM_HEADS, dtype=F32) + nrm(ks[13], (DEPTH, 2, MLSTM_HEADS), 0.1)
    return {
        "x": nrm(ks[0], (BATCH, SEQ, D), 1.0),
        "c": nrm(ks[1], (BATCH, D), 1.0),
        "ctx": nrm(ks[2], (BATCH, CTX_LEN, D), 1.0),
        "c_ctx": nrm(ks[3], (D,), 1.0),
        "norm_mix": 1.0 + nrm(ks[4], (DEPTH, D), 0.02),
        "norm_ffn": 1.0 + nrm(ks[5], (DEPTH, D), 0.02),
        "w_mod": nrm(ks[6], (DEPTH, D, 6 * D), 0.5 * D ** -0.5),
        "b_mod": nrm(ks[7], (DEPTH, 6 * D), 0.02),
        "w_in": nrm(ks[8], (DEPTH, D, D_IN), D ** -0.5),
        "ret_decay": (jnp.log(gam) - jnp.log1p(-gam)) + nrm(ks[9], (DEPTH, 2, RET_HEADS), 0.05),
        "mlstm_conv_w": nrm(ks[10], (DEPTH, MLSTM_CONV, 2 * MLSTM_W), MLSTM_CONV ** -0.5),
        "mlstm_conv_b": nrm(ks[11], (DEPTH, 2 * MLSTM_W), 0.02),
        "mlstm_gate_b": jnp.concatenate([i_b, f_b], axis=1),
        "na_q_gain": 1.0 + nrm(ks[14], (DEPTH, HEAD_DIM), 0.02),
        "na_k_gain": 1.0 + nrm(ks[15], (DEPTH, HEAD_DIM), 0.02),
        "na_rpb": nrm(ks[16], (DEPTH, NA_HEADS, 2 * NA_WIN_R - 1, 2 * NA_WIN_C - 1), 0.1),
        "w_out": nrm(ks[17], (DEPTH, D_MIX, D), D_MIX ** -0.5),
        "ffn_w_gate": nrm(ks[18], (N_DENSE, D, D_FF), D ** -0.5),
        "ffn_w_up": nrm(ks[19], (N_DENSE, D, D_FF), D ** -0.5),
        "ffn_w_down": nrm(ks[20], (N_DENSE, D_FF, D), D_FF ** -0.5),
        "moe_router": nrm(ks[21], (N_MOE, D, N_EXPERTS), D ** -0.5),
        "moe_w_gate": nrm(ks[22], (N_MOE, N_EXPERTS, D, D_FF), D ** -0.5),
        "moe_w_up": nrm(ks[23], (N_MOE, N_EXPERTS, D, D_FF), D ** -0.5),
        "moe_w_down": nrm(ks[24], (N_MOE, N_EXPERTS, D_FF, D), D_FF ** -0.5),
    }


def reference(x, c, ctx, c_ctx, norm_mix, norm_ffn, w_mod, b_mod, w_in, ret_decay, mlstm_conv_w,
              mlstm_conv_b, mlstm_gate_b, na_q_gain, na_k_gain, na_rpb, w_out, ffn_w_gate, ffn_w_up,
              ffn_w_down, moe_router, moe_w_gate, moe_w_up, moe_w_down):
    sc_x = jax.nn.silu(c)[:, None, :]
    sc_c = jax.nn.silu(c_ctx)[None, None, :]
    for l in range(DEPTH):
        need_ctx = l < DEPTH - 1
        sh1, s1, g1, sh2, s2, g2 = jnp.split(sc_x @ w_mod[l] + b_mod[l], 6, axis=-1)
        csh1, cs1, cg1, csh2, cs2, cg2 = jnp.split(sc_c @ w_mod[l] + b_mod[l], 6, axis=-1)
        hx = rmsnorm(x, norm_mix[l]) * (1.0 + s1) + sh1
        hc = rmsnorm(ctx, norm_mix[l]) * (1.0 + cs1) + csh1
        mx, mc = mixer(hx, hc, w_in[l], ret_decay[l], mlstm_conv_w[l], mlstm_conv_b[l], mlstm_gate_b[l],
                       na_q_gain[l], na_k_gain[l], na_rpb[l], w_out[l], need_ctx)
        x = x + g1 * mx
        if need_ctx:
            ctx = ctx + cg1 * mc
        h2 = rmsnorm(x, norm_ffn[l]) * (1.0 + s2) + sh2
        if l % 2 == 0:
            j = l // 2
            x = x + g2 * swiglu(h2, ffn_w_gate[j], ffn_w_up[j], ffn_w_down[j])
            if need_ctx:
                h2c = rmsnorm(ctx, norm_ffn[l]) * (1.0 + cs2) + csh2
                ctx = ctx + cg2 * swiglu(h2c, ffn_w_gate[j], ffn_w_up[j], ffn_w_down[j])
        else:
            j = l // 2
            x = x + g2 * moe_swiglu(h2, moe_router[j], moe_w_gate[j], moe_w_up[j], moe_w_down[j])
            if need_ctx:
                h2c = rmsnorm(ctx, norm_ffn[l]) * (1.0 + cs2) + csh2
                ctx = ctx + cg2 * moe_swiglu(h2c, moe_router[j], moe_w_gate[j], moe_w_up[j], moe_w_down[j])
    return x
```

```python
import functools

import numpy as np
import jax
import jax.numpy as jnp
from jax import lax
from jax.experimental import pallas as pl
from jax.experimental.pallas import tpu as pltpu

F32 = jnp.float32
BF16 = jnp.bfloat16

D_MODEL = 1024
GRID_W = 64
HEAD_DIM = 64
RET_HEADS = 4
MLSTM_HEADS = 4
NA_HEADS = 8
RET_W = RET_HEADS * HEAD_DIM
MLSTM_W = MLSTM_HEADS * HEAD_DIM
NA_W = NA_HEADS * HEAD_DIM
CHUNK = 128
NA_WIN_R = 8
NA_WIN_C = 16
ROPE_BASE = 10000.0
D_FF = 2816
N_EXPERTS = 8
EPS = 1e-6
NEG = -1e30

LANES = 128
VMEM_LIMIT = 56 * 1024 * 1024

TM = 512
TF = 1408
NA_TQ = 256
MOE_T = 1024
MOE_R = 256
E_PAD = 16


def _dot(a, b):
    return jnp.dot(a, b, preferred_element_type=F32)


def _dot_nt(a, b):
    return lax.dot_general(a, b, (((1,), (1,)), ((), ())), preferred_element_type=F32)


def _split2(x):
    hi = x.astype(BF16)
    lo = (x - hi.astype(F32)).astype(BF16)
    return hi, lo


def _split3(x):
    hi = x.astype(BF16)
    r = x - hi.astype(F32)
    mid = r.astype(BF16)
    lo = (r - mid.astype(F32)).astype(BF16)
    return hi, mid, lo


def _sigmoid(x):
    return 1.0 / (1.0 + jnp.exp(-x))


def _silu(x):
    return x * _sigmoid(x)


def _log_sigmoid(x):
    return jnp.minimum(x, 0.0) - jnp.log1p(jnp.exp(-jnp.abs(x)))


def _head_mean_sq(y, g):
    hi, lo = _split2(y * y)
    return _dot(hi, g) + _dot(lo, g)


def _params(sem):
    return pltpu.CompilerParams(dimension_semantics=sem, vmem_limit_bytes=VMEM_LIMIT)


def _mod_kernel(s_ref, w_ref, b_ref, o_ref):
    s = _silu(s_ref[...])
    shi, slo = _split2(s)
    whi, wlo = _split2(w_ref[0])
    o_ref[0] = _dot(shi, whi) + _dot(slo, whi) + _dot(shi, wlo) + b_ref[0]


def _modulation(cvecs, w_mod, b_mod):
    L = w_mod.shape[0]
    tn = 1536
    return pl.pallas_call(
        _mod_kernel,
        grid=(L, 6 * D_MODEL // tn),
        in_specs=[pl.BlockSpec((16, D_MODEL), lambda l, j: (0, 0)),
                  pl.BlockSpec((1, D_MODEL, tn), lambda l, j: (l, 0, j)),
                  pl.BlockSpec((1, 1, tn), lambda l, j: (l, 0, j))],
        out_specs=pl.BlockSpec((1, 16, tn), lambda l, j: (l, 0, j)),
        out_shape=jax.ShapeDtypeStruct((L, 16, 6 * D_MODEL), F32),
        compiler_params=_params(("parallel", "parallel")),
        name="modulation",
    )(cvecs, w_mod, b_mod.reshape(L, 1, 6 * D_MODEL))


def _norm_mod(x, nw, shift, scale):
    ms = jnp.mean(x * x, axis=-1, keepdims=True)
    return x * lax.rsqrt(ms + EPS) * nw * (1.0 + scale) + shift


def _kin_kernel(x_ref, mod_ref, nw_ref, w_ref, wgc_ref, wgr_ref, gbc_ref, gbr_ref, cos_ref, sin_ref,
                qg_ref, kg_ref, g128_ref, ret_ref, ml_ref, gc_ref, gr_ref, na_ref, *, latent):
    h = _norm_mod(x_ref[...], nw_ref[...], mod_ref[0, 0:1, :], mod_ref[0, 1:2, :])
    hb = h.astype(BF16)
    for j in range(2 * RET_W // LANES):
        sl = slice(j * LANES, (j + 1) * LANES)
        y = _dot(hb, w_ref[:, sl])
        if latent:
            lane = lax.broadcasted_iota(jnp.int32, y.shape, 1)
            partner = jnp.where((lane % 32) < 16, pltpu.roll(y, LANES - 16, 1), pltpu.roll(y, 16, 1))
            y = y * cos_ref[...] + partner * sin_ref[...]
        ret_ref[:, sl] = y
    ret_ref[:, 2 * RET_W:] = _dot(hb, w_ref[:, 2 * RET_W:4 * RET_W])
    ml_ref[...] = _dot(hb, w_ref[:, 4 * RET_W:4 * RET_W + 4 * MLSTM_W])
    gc_ref[...] = _dot(hb, wgc_ref[...])[:, :16] + gbc_ref[...]
    gr_ref[...] = _dot_nt(wgr_ref[...], hb) + gbr_ref[...]
    base = 4 * RET_W + 4 * MLSTM_W
    g128 = g128_ref[...]
    for j in range(2 * NA_W // LANES):
        y = _dot(hb, w_ref[:, base + j * LANES: base + (j + 1) * LANES])
        gain = qg_ref[...] if j < NA_W // LANES else kg_ref[...]
        y = y * lax.rsqrt(_head_mean_sq(y, g128) + EPS) * gain
        na_ref[:, j * LANES:(j + 1) * LANES] = y.astype(BF16)
    na_ref[:, 2 * NA_W:] = _dot(hb, w_ref[:, base + 2 * NA_W:]).astype(BF16)


def _in_proj(x, mod, nw, w, wgc, wgr, gbc, gbr, cos, sin, qg, kg, g128, *, latent, seq):
    n = x.shape[0]
    tiles_per_seq = seq // TM
    if latent:
        mod_map = lambda i: (1 + i // tiles_per_seq, 0, 0)
        rope_map = lambda i: (i % tiles_per_seq, 0)
    else:
        mod_map = lambda i: (0, 0, 0)
        rope_map = lambda i: (0, 0)
    wtot = w.shape[1]
    const = lambda i: (0, 0)
    return pl.pallas_call(
        functools.partial(_kin_kernel, latent=latent),
        grid=(n // TM,),
        in_specs=[pl.BlockSpec((TM, D_MODEL), lambda i: (i, 0)),
                  pl.BlockSpec((1, 6, D_MODEL), mod_map),
                  pl.BlockSpec((1, D_MODEL), const),
                  pl.BlockSpec((D_MODEL, wtot), const),
                  pl.BlockSpec((D_MODEL, LANES), const),
                  pl.BlockSpec((16, D_MODEL), const),
                  pl.BlockSpec((1, 16), const),
                  pl.BlockSpec((16, 1), const),
                  pl.BlockSpec((TM, LANES), rope_map),
                  pl.BlockSpec((TM, LANES), rope_map),
                  pl.BlockSpec((1, LANES), const),
                  pl.BlockSpec((1, LANES), const),
                  pl.BlockSpec((LANES, LANES), const)],
        out_specs=[pl.BlockSpec((TM, 4 * RET_W), lambda i: (i, 0)),
                   pl.BlockSpec((TM, 4 * MLSTM_W), lambda i: (i, 0)),
                   pl.BlockSpec((TM, 16), lambda i: (i, 0)),
                   pl.BlockSpec((16, TM), lambda i: (0, i)),
                   pl.BlockSpec((TM, 3 * NA_W), lambda i: (i, 0))],
        out_shape=[jax.ShapeDtypeStruct((n, 4 * RET_W), F32),
                   jax.ShapeDtypeStruct((n, 4 * MLSTM_W), F32),
                   jax.ShapeDtypeStruct((n, 16), F32),
                   jax.ShapeDtypeStruct((16, n), F32),
                   jax.ShapeDtypeStruct((n, 3 * NA_W), BF16)],
        compiler_params=_params(("parallel",)),
        name="in_proj_lat" if latent else "in_proj_ctx",
    )(x, mod, nw, w, wgc, wgr, gbc, gbr, cos, sin, qg, kg, g128)


def _lane_head_masks(width):
    lane = lax.broadcasted_iota(jnp.int32, (1, width), 1)
    return [(lane // HEAD_DIM) == h for h in range(width // HEAD_DIM)]


def _block_diag_mask(width):
    r = lax.broadcasted_iota(jnp.int32, (width, width), 0) // HEAD_DIM
    c = lax.broadcasted_iota(jnp.int32, (width, width), 1) // HEAD_DIM
    return r == c


def _chunk_order(n_lat, d):
    return (lambda k: k) if d == 0 else (lambda k: n_lat - 1 - k)


def _ret_kernel(rc_ref, rl_ref, rdl_ref, rdh_ref, *out_refs, with_ctx):
    if with_ctx:
        oc_ref, ol_ref = out_refs
    else:
        (ol_ref,) = out_refs
        oc_ref = None
    n_ctx = rc_ref.shape[0] // CHUNK
    n_lat = rl_ref.shape[0] // CHUNK
    W = RET_W
    hm = _lane_head_masks(W)
    bd = _block_diag_mask(W)
    gmat = jnp.where(bd, 1.0 / HEAD_DIM, 0.0).astype(BF16)
    lgl = _log_sigmoid(rdl_ref[...])
    lgh = _log_sigmoid(rdh_ref[...])
    ii = lax.broadcasted_iota(jnp.int32, (CHUNK, CHUNK), 0)
    jj = lax.broadcasted_iota(jnp.int32, (CHUNK, CHUNK), 1)
    adiff = jnp.abs(ii - jj).astype(F32)
    idx = lax.broadcasted_iota(jnp.int32, (CHUNK, 1), 0).astype(F32)

    for d in range(2):
        pos = idx if d == 0 else (CHUNK - 1.0) - idx
        lg = lgl[d:d + 1, :]
        qdec = jnp.exp(lg * (pos + 1.0))
        kdec = jnp.exp(lg * ((CHUNK - 1.0) - pos))
        cdec = jnp.exp(lg * float(CHUNK))
        keep = (ii >= jj) if d == 0 else (jj >= ii)
        dmats = [jnp.where(keep, jnp.exp(lgh[d * RET_HEADS + h:d * RET_HEADS + h + 1, :] * adiff), 0.0)
                 for h in range(RET_HEADS)]

        def step(S, src_ref, dst_ref, r0, emit):
            blk = src_ref[pl.ds(r0, CHUNK), :]
            q = blk[:, 0:W]
            k = blk[:, W:2 * W] * (HEAD_DIM ** -0.5)
            vb = blk[:, 2 * W:3 * W].astype(BF16)
            if emit:
                kb = k.astype(BF16)
                y = _dot((q * qdec).astype(BF16), S.astype(BF16))
                for h in range(RET_HEADS):
                    att = _dot_nt(jnp.where(hm[h], q, 0.0).astype(BF16), kb) * dmats[h]
                    y = y + jnp.where(hm[h], _dot(att.astype(BF16), vb), 0.0)
                if d == 0:
                    dst_ref[pl.ds(r0, CHUNK), :] = y
                else:
                    y = y + dst_ref[pl.ds(r0, CHUNK), :]
                    g = blk[:, 3 * W:4 * W]
                    y = y * lax.rsqrt(_head_mean_sq(y, gmat) + EPS)
                    dst_ref[pl.ds(r0, CHUNK), :] = _silu(g) * y
            kd = (k * kdec).T.astype(BF16)
            return cdec * S + jnp.where(bd, _dot(kd, vb), 0.0)

        S = jnp.zeros((W, W), F32)
        for k in range(n_ctx):
            c = k if d == 0 else n_ctx - 1 - k
            S = step(S, rc_ref, oc_ref, c * CHUNK, with_ctx)
        order = _chunk_order(n_lat, d)

        def body(k, S):
            r0 = pl.multiple_of(order(k) * CHUNK, CHUNK)
            return step(S, rl_ref, ol_ref, r0, True)

        lax.fori_loop(0, n_lat, body, S)


def _retention(ret_c, ret_l, rdl, rdh, *, batch, with_ctx):
    lc = ret_c.shape[0] // batch
    ll = ret_l.shape[0] // batch
    out_specs = [pl.BlockSpec((ll, RET_W), lambda b: (b, 0))]
    out_shape = [jax.ShapeDtypeStruct((ret_l.shape[0], RET_W), F32)]
    if with_ctx:
        out_specs = [pl.BlockSpec((lc, RET_W), lambda b: (b, 0))] + out_specs
        out_shape = [jax.ShapeDtypeStruct((ret_c.shape[0], RET_W), F32)] + out_shape
    return pl.pallas_call(
        functools.partial(_ret_kernel, with_ctx=with_ctx),
        grid=(batch,),
        in_specs=[pl.BlockSpec((lc, 4 * RET_W), lambda b: (b, 0)),
                  pl.BlockSpec((ll, 4 * RET_W), lambda b: (b, 0)),
                  pl.BlockSpec((2, RET_W), lambda b: (0, 0)),
                  pl.BlockSpec((2 * RET_HEADS, LANES), lambda b: (0, 0))],
        out_specs=out_specs,
        out_shape=out_shape,
        compiler_params=_params(("parallel",)),
        name="retention",
    )(ret_c, ret_l, rdl, rdh)


def _ml_kernel(mc_ref, mlat_ref, gcc_ref, gcl_ref, grc_ref, grl_ref, cw_ref, cb_ref, *rest, with_ctx):
    if with_ctx:
        oc_ref, ol_ref, pad_ref, qkc_ref, qkl_ref = rest
    else:
        ol_ref, pad_ref, qkc_ref, qkl_ref = rest
        oc_ref = None
    n_ctx = mc_ref.shape[0] // CHUNK
    n_lat = mlat_ref.shape[0] // CHUNK
    W = MLSTM_W
    H = MLSTM_HEADS
    hm = _lane_head_masks(W)
    bd = _block_diag_mask(W)
    gmat = jnp.where(bd, 1.0 / HEAD_DIM, 0.0).astype(BF16)
    ones_bd = jnp.where(bd, 1.0, 0.0).astype(BF16)
    ii = lax.broadcasted_iota(jnp.int32, (CHUNK, CHUNK), 0)
    jj = lax.broadcasted_iota(jnp.int32, (CHUNK, CHUNK), 1)
    tri_lo = jnp.where(ii >= jj, 1.0, 0.0).astype(BF16)
    tri_up = jnp.where(jj >= ii, 1.0, 0.0).astype(BF16)

    w0, w1, w2 = cw_ref[0:1, :], cw_ref[1:2, :], cw_ref[2:3, :]
    cb = cb_ref[...]

    def conv_seq(src_ref, dst_ref, n_chunks):
        rows = n_chunks * CHUNK
        zero8 = jnp.zeros((8, 2 * W), F32)
        pad_ref[0:8, :] = zero8
        pad_ref[pl.ds(8 + rows, 8), :] = zero8

        def cp(k, _):
            r0 = pl.multiple_of(k * CHUNK, CHUNK)
            pad_ref[pl.ds(r0 + 8, CHUNK), :] = src_ref[pl.ds(r0, CHUNK), 0:2 * W]
            return 0

        lax.fori_loop(0, n_chunks, cp, 0)

        def cv(k, _):
            r0 = pl.multiple_of(k * CHUNK, CHUNK)
            xm = pad_ref[pl.ds(r0, CHUNK + 16), :]
            prev = pltpu.roll(xm, 1, 0)[8:8 + CHUNK]
            nxt = pltpu.roll(xm, CHUNK + 15, 0)[8:8 + CHUNK]
            y = w0 * prev + w1 * xm[8:8 + CHUNK] + w2 * nxt + cb
            dst_ref[pl.ds(r0, CHUNK), :] = _silu(y)
            return 0

        lax.fori_loop(0, n_chunks, cv, 0)

    conv_seq(mc_ref, qkc_ref, n_ctx)
    conv_seq(mlat_ref, qkl_ref, n_lat)

    for d in range(2):
        tri_c = tri_lo if d == 0 else tri_up
        tri_r = tri_up if d == 0 else tri_lo
        keep = (jj <= ii) if d == 0 else (jj >= ii)
        last = CHUNK - 1 if d == 0 else 0

        def step(carry, src_ref, qk_ref, gc_ref, gr_ref, dst_ref, r0, emit):
            C, n, ms = carry
            blk = src_ref[pl.ds(r0, CHUNK), :]
            qk = qk_ref[pl.ds(r0, CHUNK), :]
            q = qk[:, 0:W]
            k = qk[:, W:2 * W] * (HEAD_DIM ** -0.5)
            vb = blk[:, 2 * W:3 * W].astype(BF16)
            gcol = gc_ref[pl.ds(r0, CHUNK), :]
            grow = gr_ref[:, pl.ds(r0, CHUNK)]
            lf_c = _log_sigmoid(gcol)
            lf_r = _log_sigmoid(grow)
            c_hi, c_mid, c_lo = _split3(lf_c)
            b_col = _dot(tri_c, c_hi) + _dot(tri_c, c_mid) + _dot(tri_c, c_lo)
            r_hi, r_mid, r_lo = _split3(lf_r)
            b_row = _dot(r_hi, tri_r) + _dot(r_mid, tri_r) + _dot(r_lo, tri_r)
            b_last = b_col[last:last + 1, :]

            wk_l = jnp.zeros((CHUNK, W), F32)
            dp_l = jnp.zeros((1, W), F32)
            if emit:
                qb = q.astype(BF16)
                kb = k.astype(BF16)
                num = jnp.zeros((CHUNK, W), F32)
                den = jnp.zeros((CHUNK, W), F32)
                gi_l = jnp.zeros((CHUNK, W), F32)
                em_l = jnp.zeros((CHUNK, W), F32)
            new_ms = []
            for h in range(H):
                cf = 2 * H + d * H + h
                ci = d * H + h
                bc = b_col[:, cf:cf + 1]
                bl = b_last[:, cf:cf + 1]
                ic = gcol[:, ci:ci + 1]
                m_h = ms[h]
                ws = bl - bc + ic
                m_new = jnp.maximum(bl + m_h, jnp.max(ws, axis=0, keepdims=True))
                dprev = jnp.exp(bl + m_h - m_new)
                wk = jnp.exp(ws - m_new)
                wk_l = jnp.where(hm[h], wk, wk_l)
                dp_l = jnp.where(hm[h], dprev, dp_l)
                new_ms.append(m_new)
                if emit:
                    br = b_row[cf:cf + 1, :]
                    ir = grow[ci:ci + 1, :]
                    dm = jnp.where(keep, bc + (ir - br), -jnp.inf)
                    inter = bc + m_h
                    mrow = jnp.maximum(jnp.max(dm, axis=-1, keepdims=True), inter)
                    e = jnp.exp(dm - mrow)
                    gi = jnp.exp(inter - mrow)
                    s = _dot_nt(jnp.where(hm[h], qb, jnp.zeros_like(qb)), kb) * e
                    num = num + jnp.where(hm[h], _dot(s.astype(BF16), vb), 0.0)
                    den = jnp.where(hm[h], jnp.sum(s, axis=-1, keepdims=True), den)
                    gi_l = jnp.where(hm[h], gi, gi_l)
                    em_l = jnp.where(hm[h], jnp.exp(-mrow), em_l)
            if emit:
                qc = _dot(qb, C.astype(BF16))
                n_hi, n_lo = _split2(q * n)
                qn = _dot(n_hi, ones_bd) + _dot(n_lo, ones_bd)
                num = num + gi_l * qc
                den = den + gi_l * qn
                hout = num / jnp.maximum(jnp.abs(den), em_l)
                if d == 0:
                    dst_ref[pl.ds(r0, CHUNK), :] = hout
                else:
                    y = hout + dst_ref[pl.ds(r0, CHUNK), :]
                    o = blk[:, 3 * W:4 * W]
                    y = y * lax.rsqrt(_head_mean_sq(y, gmat) + EPS)
                    dst_ref[pl.ds(r0, CHUNK), :] = _sigmoid(o) * y
            kw = k * wk_l
            C = dp_l * C + jnp.where(bd, _dot(kw.T.astype(BF16), vb), 0.0)
            n = dp_l * n + jnp.sum(kw, axis=0, keepdims=True)
            return C, n, tuple(new_ms)

        carry = (jnp.zeros((W, W), F32), jnp.zeros((1, W), F32),
                 tuple(jnp.zeros((1, 1), F32) for _ in range(H)))
        for k in range(n_ctx):
            c = k if d == 0 else n_ctx - 1 - k
            carry = step(carry, mc_ref, qkc_ref, gcc_ref, grc_ref, oc_ref, c * CHUNK, with_ctx)
        order = _chunk_order(n_lat, d)

        def body(k, carry):
            r0 = pl.multiple_of(order(k) * CHUNK, CHUNK)
            return step(carry, mlat_ref, qkl_ref, gcl_ref, grl_ref, ol_ref, r0, True)

        lax.fori_loop(0, n_lat, body, carry)


def _mlstm(ml_c, ml_l, gc_c, gc_l, gr_c, gr_l, conv_w, conv_b, *, batch, with_ctx):
    lc = ml_c.shape[0] // batch
    ll = ml_l.shape[0] // batch
    out_specs = [pl.BlockSpec((ll, MLSTM_W), lambda b: (b, 0))]
    out_shape = [jax.ShapeDtypeStruct((ml_l.shape[0], MLSTM_W), F32)]
    if with_ctx:
        out_specs = [pl.BlockSpec((lc, MLSTM_W), lambda b: (b, 0))] + out_specs
        out_shape = [jax.ShapeDtypeStruct((ml_c.shape[0], MLSTM_W), F32)] + out_shape
    return pl.pallas_call(
        functools.partial(_ml_kernel, with_ctx=with_ctx),
        grid=(batch,),
        in_specs=[pl.BlockSpec((lc, 4 * MLSTM_W), lambda b: (b, 0)),
                  pl.BlockSpec((ll, 4 * MLSTM_W), lambda b: (b, 0)),
                  pl.BlockSpec((lc, 16), lambda b: (b, 0)),
                  pl.BlockSpec((ll, 16), lambda b: (b, 0)),
                  pl.BlockSpec((16, lc), lambda b: (0, b)),
                  pl.BlockSpec((16, ll), lambda b: (0, b)),
                  pl.BlockSpec((3, 2 * MLSTM_W), lambda b: (0, 0)),
                  pl.BlockSpec((1, 2 * MLSTM_W), lambda b: (0, 0))],
        out_specs=out_specs,
        out_shape=out_shape,
        scratch_shapes=[pltpu.VMEM((ll + 16, 2 * MLSTM_W), F32),
                        pltpu.VMEM((lc, 2 * MLSTM_W), F32),
                        pltpu.VMEM((ll, 2 * MLSTM_W), F32)],
        compiler_params=_params(("parallel",)),
        name="mlstm",
    )(ml_c, ml_l, gc_c, gc_l, gr_c, gr_l, conv_w, conv_b)


def _attn_pairs(q, k_tiles, v_tiles, bias_fn, o_ref):
    scale = HEAD_DIM ** -0.5
    n_pairs = q.shape[1] // LANES
    lane = lax.broadcasted_iota(jnp.int32, (1, LANES), 1)
    for p in range(n_pairs):
        sl = slice(p * LANES, (p + 1) * LANES)
        qp = q[:, sl]
        kp = [kt[:, sl] for kt in k_tiles]
        vp = [vt[:, sl] for vt in v_tiles]
        o_pair = jnp.zeros((q.shape[0], LANES), F32)
        for hh in range(2):
            mask = (lane < HEAD_DIM) if hh == 0 else (lane >= HEAD_DIM)
            qm = jnp.where(mask, qp, jnp.zeros_like(qp))
            s = []
            for i, kt in enumerate(kp):
                si = _dot_nt(qm, kt) * scale
                b = bias_fn(2 * p + hh, i)
                s.append(si if b is None else si + b)
            m = s[0].max(axis=-1, keepdims=True)
            for si in s[1:]:
                m = jnp.maximum(m, si.max(axis=-1, keepdims=True))
            l = jnp.zeros_like(m)
            o = jnp.zeros((q.shape[0], LANES), F32)
            for si, vt in zip(s, vp):
                pi = jnp.exp(si - m)
                l = l + pi.sum(axis=-1, keepdims=True)
                o = o + _dot(pi.astype(BF16), vt)
            o_pair = jnp.where(mask, o / l, o_pair)
        o_ref[:, sl] = o_pair.astype(o_ref.dtype)


def _na_kernel(q_ref, k0_ref, k1_ref, k2_ref, v0_ref, v1_ref, v2_ref, kc_ref, vc_ref, bias_ref, o_ref):
    def bias_fn(h, i):
        if i == 3:
            return None
        return bias_ref[0, h, :, i * NA_TQ:(i + 1) * NA_TQ]

    _attn_pairs(q_ref[...], [k0_ref[...], k1_ref[...], k2_ref[...], kc_ref[...]],
                [v0_ref[...], v1_ref[...], v2_ref[...], vc_ref[...]], bias_fn, o_ref)


def _neighbourhood_attention(na_l, na_c, bias, *, batch):
    n = na_l.shape[0]
    seq = n // batch
    nt = seq // NA_TQ
    band0 = lambda t: jnp.clip(t - 1, 0, nt - 3)
    cls = lambda t: jnp.where(t == 0, 0, jnp.where(t == nt - 1, 2, 1))
    blk = (NA_TQ, NA_W)

    def kv_spec(i, col):
        return pl.BlockSpec(blk, lambda t, b: (b * nt + band0(t) + i, col))

    return pl.pallas_call(
        _na_kernel,
        grid=(nt, batch),
        in_specs=[pl.BlockSpec(blk, lambda t, b: (b * nt + t, 0)),
                  kv_spec(0, 1), kv_spec(1, 1), kv_spec(2, 1),
                  kv_spec(0, 2), kv_spec(1, 2), kv_spec(2, 2),
                  pl.BlockSpec(blk, lambda t, b: (b, 1)),
                  pl.BlockSpec(blk, lambda t, b: (b, 2)),
                  pl.BlockSpec((1, NA_HEADS, NA_TQ, 3 * NA_TQ), lambda t, b: (cls(t), 0, 0, 0))],
        out_specs=pl.BlockSpec(blk, lambda t, b: (b * nt + t, 0)),
        out_shape=jax.ShapeDtypeStruct((n, NA_W), BF16),
        compiler_params=_params(("arbitrary", "arbitrary")),
        name="neighbourhood_attention",
    )(na_l, na_l, na_l, na_l, na_l, na_l, na_l, na_c, na_c, bias)


def _ctx_attn_kernel(q_ref, k_ref, v_ref, o_ref):
    _attn_pairs(q_ref[...], [k_ref[...]], [v_ref[...]], lambda h, i: None, o_ref)


def _context_attention(na_c, *, batch):
    n = na_c.shape[0]
    lc = n // batch
    blk = (lc, NA_W)
    return pl.pallas_call(
        _ctx_attn_kernel,
        grid=(batch,),
        in_specs=[pl.BlockSpec(blk, lambda b: (b, 0)),
                  pl.BlockSpec(blk, lambda b: (b, 1)),
                  pl.BlockSpec(blk, lambda b: (b, 2))],
        out_specs=pl.BlockSpec(blk, lambda b: (b, 0)),
        out_shape=jax.ShapeDtypeStruct((n, NA_W), BF16),
        compiler_params=_params(("parallel",)),
        name="context_attention",
    )(na_c, na_c, na_c)


def _na_bias_tables(rpb, seq):
    rows = seq // GRID_W
    rq = NA_TQ // GRID_W
    tabs = []
    for r0, bs in ((0, 0), (2 * rq, rq), (rows - rq, rows - 3 * rq)):
        qi = np.arange(NA_TQ)
        kj = np.arange(3 * NA_TQ)
        qr, qc = r0 + qi // GRID_W, qi % GRID_W
        kr, kc = bs + kj // GRID_W, kj % GRID_W
        sr = np.clip(qr - NA_WIN_R // 2, 0, rows - NA_WIN_R)
        sc = np.clip(qc - NA_WIN_C // 2, 0, GRID_W - NA_WIN_C)
        ok = ((kr[None, :] >= sr[:, None]) & (kr[None, :] < sr[:, None] + NA_WIN_R)
              & (kc[None, :] >= sc[:, None]) & (kc[None, :] < sc[:, None] + NA_WIN_C))
        dr = np.clip(kr[None, :] - qr[:, None] + NA_WIN_R - 1, 0, 2 * NA_WIN_R - 2)
        dc = np.clip(kc[None, :] - qc[:, None] + NA_WIN_C - 1, 0, 2 * NA_WIN_C - 2)
        tabs.append(jnp.where(ok[None], rpb.astype(F32)[:, dr, dc], NEG))
    return jnp.stack(tabs)


def _out_kernel(x_ref, mod_ref, r_ref, m_ref, a_ref, w_ref, o_ref):
    cat = jnp.concatenate([r_ref[...].astype(BF16), m_ref[...].astype(BF16), a_ref[...]], axis=-1)
    o_ref[...] = x_ref[...] + mod_ref[0, 2:3, :] * _dot(cat, w_ref[...])


def _out_proj(x, mod, ret_y, ml_y, na_y, w, *, latent, seq):
    n = x.shape[0]
    tiles_per_seq = seq // TM
    mod_map = (lambda i: (1 + i // tiles_per_seq, 0, 0)) if latent else (lambda i: (0, 0, 0))
    return pl.pallas_call(
        _out_kernel,
        grid=(n // TM,),
        in_specs=[pl.BlockSpec((TM, D_MODEL), lambda i: (i, 0)),
                  pl.BlockSpec((1, 6, D_MODEL), mod_map),
                  pl.BlockSpec((TM, RET_W), lambda i: (i, 0)),
                  pl.BlockSpec((TM, MLSTM_W), lambda i: (i, 0)),
                  pl.BlockSpec((TM, NA_W), lambda i: (i, 0)),
                  pl.BlockSpec((D_MODEL, D_MODEL), lambda i: (0, 0))],
        out_specs=pl.BlockSpec((TM, D_MODEL), lambda i: (i, 0)),
        out_shape=jax.ShapeDtypeStruct((n, D_MODEL), F32),
        compiler_params=_params(("parallel",)),
        name="out_proj",
    )(x, mod, ret_y, ml_y, na_y, w)


def _ffn_kernel(x_ref, mod_ref, nw_ref, wg_ref, wu_ref, wd_ref, o_ref, h_ref):
    f = pl.program_id(1)

    @pl.when(f == 0)
    def _():
        h = _norm_mod(x_ref[...], nw_ref[...], mod_ref[0, 3:4, :], mod_ref[0, 4:5, :])
        h_ref[...] = h.astype(BF16)

    hb = h_ref[...]
    a = _dot(hb, wg_ref[...])
    u = _dot(hb, wu_ref[...])
    y = _dot((_silu(a) * u).astype(BF16), wd_ref[...])

    @pl.when(f == 0)
    def _():
        o_ref[...] = y

    @pl.when(f > 0)
    def _():
        o_ref[...] += y

    @pl.when(f == pl.num_programs(1) - 1)
    def _():
        o_ref[...] = x_ref[...] + mod_ref[0, 5:6, :] * o_ref[...]


def _dense_ffn(x, mod, nw, wg, wu, wd, *, latent, seq):
    n = x.shape[0]
    tiles_per_seq = seq // TM
    mod_map = (lambda i, f: (1 + i // tiles_per_seq, 0, 0)) if latent else (lambda i, f: (0, 0, 0))
    return pl.pallas_call(
        _ffn_kernel,
        grid=(n // TM, D_FF // TF),
        in_specs=[pl.BlockSpec((TM, D_MODEL), lambda i, f: (i, 0)),
                  pl.BlockSpec((1, 6, D_MODEL), mod_map),
                  pl.BlockSpec((1, D_MODEL), lambda i, f: (0, 0)),
                  pl.BlockSpec((D_MODEL, TF), lambda i, f: (0, f)),
                  pl.BlockSpec((D_MODEL, TF), lambda i, f: (0, f)),
                  pl.BlockSpec((TF, D_MODEL), lambda i, f: (f, 0))],
        out_specs=pl.BlockSpec((TM, D_MODEL), lambda i, f: (i, 0)),
        out_shape=jax.ShapeDtypeStruct((n, D_MODEL), F32),
        scratch_shapes=[pltpu.VMEM((TM, D_MODEL), BF16)],
        compiler_params=_params(("parallel", "arbitrary")),
        name="dense_ffn",
    )(x, mod, nw, wg, wu, wd)


def _moe_kernel(x_ref, mod_ref, nw_ref, wr_ref, wg_ref, wu_ref, wd_ref, o_ref,
                h_ref, sel_ref, pos_ref, cmb_ref, xe_ref, y_ref, tri_ref):
    i, e, f = pl.program_id(0), pl.program_id(1), pl.program_id(2)
    T, R = MOE_T, MOE_R
    last_e = pl.num_programs(1) - 1
    last_f = pl.num_programs(2) - 1

    @pl.when((i == 0) & (e == 0) & (f == 0))
    def _():
        a = lax.broadcasted_iota(jnp.int32, (T, T), 0)
        b = lax.broadcasted_iota(jnp.int32, (T, T), 1)
        tri_ref[...] = jnp.where(a < b, 1.0, 0.0).astype(BF16)

    @pl.when((e == 0) & (f == 0))
    def _():
        h = _norm_mod(x_ref[...], nw_ref[...], mod_ref[0, 3:4, :], mod_ref[0, 4:5, :])
        h_ref[...] = h.astype(BF16)
        hhi, hlo = _split2(h)
        whi, wlo = _split2(wr_ref[...])
        logit = _dot_nt(whi, hhi) + _dot_nt(whi, hlo) + _dot_nt(wlo, hhi)
        eid = lax.broadcasted_iota(jnp.int32, (E_PAD, T), 0)
        logit = jnp.where(eid < N_EXPERTS, logit, -jnp.inf)
        m1 = jnp.max(logit, axis=0, keepdims=True)
        i1 = jnp.min(jnp.where(logit == m1, eid, E_PAD), axis=0, keepdims=True)
        rest = jnp.where(eid == i1, -jnp.inf, logit)
        m2 = jnp.max(rest, axis=0, keepdims=True)
        i2 = jnp.min(jnp.where(rest == m2, eid, E_PAD), axis=0, keepdims=True)
        e2 = jnp.exp(m2 - m1)
        w1 = 1.0 / (1.0 + e2)
        w2 = e2 / (1.0 + e2)
        sel = jnp.where((eid == i1) | (eid == i2), 1.0, 0.0)
        cmb = jnp.where(eid == i1, w1, jnp.where(eid == i2, w2, 0.0))
        pos = _dot(sel.astype(BF16), tri_ref[...])
        for k in range(N_EXPERTS):
            sel_ref[k] = sel[k:k + 1, :]
            pos_ref[k] = pos[k:k + 1, :]
            cmb_ref[k] = cmb[k:k + 1, :]
        o_ref[...] = jnp.zeros_like(o_ref)

    sel_row = sel_ref[e]
    pos_row = pos_ref[e]
    cmb_row = cmb_ref[e]
    count = jnp.sum(sel_row).astype(jnp.int32)
    n_chunks = (count + (R - 1)) // R

    def onehot(r):
        slot = (lax.broadcasted_iota(jnp.int32, (R, T), 0) + r * R).astype(F32)
        return jnp.where((pos_row == slot) & (sel_row > 0.0), 1.0, 0.0)

    def chunk(r, _):
        rows = pl.ds(pl.multiple_of(r * R, R), R)

        @pl.when(f == 0)
        def _():
            xe_ref[rows, :] = _dot(onehot(r).astype(BF16), h_ref[...]).astype(BF16)

        xr = xe_ref[rows, :]
        a = _dot(xr, wg_ref[0])
        u = _dot(xr, wu_ref[0])
        y = _dot((_silu(a) * u).astype(BF16), wd_ref[0])

        @pl.when(f == 0)
        def _():
            y_ref[rows, :] = y

        @pl.when(f > 0)
        def _():
            y_ref[rows, :] += y

        @pl.when(f == last_f)
        def _():
            p = onehot(r)
            w = jnp.sum(p * cmb_row, axis=-1, keepdims=True)
            yhi, ylo = _split2(y_ref[rows, :] * w)
            pt = p.T.astype(BF16)
            o_ref[...] += _dot(pt, yhi) + _dot(pt, ylo)

        return 0

    lax.fori_loop(0, n_chunks, chunk, 0)

    @pl.when((e == last_e) & (f == last_f))
    def _():
        o_ref[...] = x_ref[...] + mod_ref[0, 5:6, :] * o_ref[...]


def _moe_ffn(x, mod, nw, wr, wg, wu, wd, *, seq):
    n = x.shape[0]
    T = MOE_T
    tiles_per_seq = seq // T
    return pl.pallas_call(
        _moe_kernel,
        grid=(n // T, N_EXPERTS, D_FF // TF),
        in_specs=[pl.BlockSpec((T, D_MODEL), lambda i, e, f: (i, 0)),
                  pl.BlockSpec((1, 6, D_MODEL), lambda i, e, f: (1 + i // tiles_per_seq, 0, 0)),
                  pl.BlockSpec((1, D_MODEL), lambda i, e, f: (0, 0)),
                  pl.BlockSpec((E_PAD, D_MODEL), lambda i, e, f: (0, 0)),
                  pl.BlockSpec((1, D_MODEL, TF), lambda i, e, f: (e, 0, f)),
                  pl.BlockSpec((1, D_MODEL, TF), lambda i, e, f: (e, 0, f)),
                  pl.BlockSpec((1, TF, D_MODEL), lambda i, e, f: (e, f, 0))],
        out_specs=pl.BlockSpec((T, D_MODEL), lambda i, e, f: (i, 0)),
        out_shape=jax.ShapeDtypeStruct((n, D_MODEL), F32),
        scratch_shapes=[pltpu.VMEM((T, D_MODEL), BF16),
                        pltpu.VMEM((N_EXPERTS, 1, T), F32),
                        pltpu.VMEM((N_EXPERTS, 1, T), F32),
                        pltpu.VMEM((N_EXPERTS, 1, T), F32),
                        pltpu.VMEM((T, D_MODEL), BF16),
                        pltpu.VMEM((T, D_MODEL), F32),
                        pltpu.VMEM((T, T), BF16)],
        compiler_params=_params(("arbitrary", "arbitrary", "arbitrary")),
        name="moe_ffn",
    )(x, mod, nw, wr, wg, wu, wd)


def _rope_tables(seq):
    t = np.arange(seq)
    half = HEAD_DIM // 2
    inv = ROPE_BASE ** (-np.arange(0, half, 2, dtype=np.float32) / half)
    ang_r = (t // GRID_W).astype(np.float32)[:, None] * inv[None, :]
    ang_c = (t % GRID_W).astype(np.float32)[:, None] * inv[None, :]
    ang = jnp.asarray(np.concatenate([ang_r, ang_r, ang_c, ang_c], axis=1), F32)
    sign = np.concatenate([-np.ones(16), np.ones(16), -np.ones(16), np.ones(16)]).astype(np.float32)
    cos = jnp.cos(ang)
    sin = jnp.sin(ang) * sign[None, :]
    return jnp.tile(cos, (1, LANES // HEAD_DIM)), jnp.tile(sin, (1, LANES // HEAD_DIM))


def _split_w_in(w):
    a = 4 * RET_W + 4 * MLSTM_W
    main = jnp.concatenate([w[:, :a], w[:, a + 16:]], axis=1).astype(BF16)
    gates = w[:, a:a + 16]
    wgc = jnp.pad(gates, ((0, 0), (0, LANES - 16))).astype(BF16)
    wgr = gates.T.astype(BF16)
    return main, wgc, wgr


def kernel(x, c, ctx, c_ctx, norm_mix, norm_ffn, w_mod, b_mod, w_in, ret_decay, mlstm_conv_w, mlstm_conv_b,
           mlstm_gate_b, na_q_gain, na_k_gain, na_rpb, w_out, ffn_w_gate, ffn_w_up, ffn_w_down, moe_router,
           moe_w_gate, moe_w_up, moe_w_down):
    B, seq, D = x.shape
    lctx = ctx.shape[1]
    depth = w_in.shape[0]
    assert D == D_MODEL and B + 1 <= 16 and seq % TM == 0 and lctx % CHUNK == 0 and lctx == NA_TQ

    cvecs = jnp.zeros((16, D), F32).at[0].set(c_ctx).at[1:B + 1].set(c)
    mod = _modulation(cvecs, w_mod, b_mod).reshape(depth, 16, 6, D)

    cos, sin = _rope_tables(seq)
    g128 = jnp.asarray(np.kron(np.eye(LANES // HEAD_DIM), np.full((HEAD_DIM, HEAD_DIM), 1.0 / HEAD_DIM)), BF16)

    xl = x.reshape(B * seq, D)
    xc = ctx.reshape(B * lctx, D)
    for l in range(depth):
        need_ctx = l < depth - 1
        w_main, wgc, wgr = _split_w_in(w_in[l])
        gbc = mlstm_gate_b[l].reshape(1, 16).astype(F32)
        gbr = gbc.reshape(16, 1)
        qg = jnp.tile(na_q_gain[l].astype(F32), LANES // HEAD_DIM)[None, :]
        kg = jnp.tile(na_k_gain[l].astype(F32), LANES // HEAD_DIM)[None, :]
        nw_mix = norm_mix[l].astype(F32)[None, :]
        nw_ffn = norm_ffn[l].astype(F32)[None, :]
        prep = functools.partial(_in_proj, mod=mod[l], nw=nw_mix, w=w_main, wgc=wgc, wgr=wgr, gbc=gbc, gbr=gbr,
                                 cos=cos, sin=sin, qg=qg, kg=kg, g128=g128, seq=seq)
        ret_l, ml_l, gc_l, gr_l, na_l = prep(xl, latent=True)
        ret_c, ml_c, gc_c, gr_c, na_c = prep(xc, latent=False)

        rd = ret_decay[l].astype(F32)
        rdl = jnp.repeat(rd, HEAD_DIM, axis=1)
        rdh = jnp.broadcast_to(rd.reshape(2 * RET_HEADS, 1), (2 * RET_HEADS, LANES))
        ret_y = _retention(ret_c, ret_l, rdl, rdh, batch=B, with_ctx=need_ctx)
        ml_y = _mlstm(ml_c, ml_l, gc_c, gc_l, gr_c, gr_l, mlstm_conv_w[l].astype(F32),
                      mlstm_conv_b[l].astype(F32)[None, :], batch=B, with_ctx=need_ctx)
        bias = _na_bias_tables(na_rpb[l], seq)
        na_y = _neighbourhood_attention(na_l, na_c, bias, batch=B)
        wo = w_out[l].astype(BF16)
        if need_ctx:
            na_yc = _context_attention(na_c, batch=B)
            xc = _out_proj(xc, mod[l], ret_y[0], ml_y[0], na_yc, wo, latent=False, seq=seq)
        xl = _out_proj(xl, mod[l], ret_y[-1], ml_y[-1], na_y, wo, latent=True, seq=seq)

        j = l // 2
        if l % 2 == 0:
            wg, wu, wd = ffn_w_gate[j].astype(BF16), ffn_w_up[j].astype(BF16), ffn_w_down[j].astype(BF16)
            xl = _dense_ffn(xl, mod[l], nw_ffn, wg, wu, wd, latent=True, seq=seq)
            if need_ctx:
                xc = _dense_ffn(xc, mod[l], nw_ffn, wg, wu, wd, latent=False, seq=seq)
        else:
            if need_ctx:
                raise NotImplementedError("context stream through a MoE layer")
            wr = jnp.pad(moe_router[j].astype(F32).T, ((0, E_PAD - N_EXPERTS), (0, 0)))
            xl = _moe_ffn(xl, mod[l], nw_ffn, wr, moe_w_gate[j].astype(BF16), moe_w_up[j].astype(BF16),
                          moe_w_down[j].astype(BF16), seq=seq)
    return xl.reshape(B, seq, D)
```

```python
import functools

import numpy as np
import jax
import jax.numpy as jnp
from jax import lax
from jax.experimental import pallas as pl
from jax.experimental.pallas import tpu as pltpu

F32 = jnp.float32
BF16 = jnp.bfloat16

D_MODEL = 1024
GRID_W = 64
HEAD_DIM = 64
RET_HEADS = 4
MLSTM_HEADS = 4
NA_HEADS = 8
RET_W = RET_HEADS * HEAD_DIM
MLSTM_W = MLSTM_HEADS * HEAD_DIM
NA_W = NA_HEADS * HEAD_DIM
CHUNK = 128
NA_WIN_R = 8
NA_WIN_C = 16
ROPE_BASE = 10000.0
D_FF = 2816
N_EXPERTS = 8
EPS = 1e-6
NEG = -1e30

LANES = 128
VMEM_LIMIT = 56 * 1024 * 1024

TM = 512
TF = 1408
NA_TQ = 256
MOE_T = 1024
MOE_R = 256
E_PAD = 16


def _dot(a, b):
    return jnp.dot(a, b, preferred_element_type=F32)


def _dot_nt(a, b):
    return lax.dot_general(a, b, (((1,), (1,)), ((), ())), preferred_element_type=F32)


def _split2(x):
    hi = x.astype(BF16)
    lo = (x - hi.astype(F32)).astype(BF16)
    return hi, lo


def _split3(x):
    hi = x.astype(BF16)
    r = x - hi.astype(F32)
    mid = r.astype(BF16)
    lo = (r - mid.astype(F32)).astype(BF16)
    return hi, mid, lo


def _sigmoid(x):
    return 1.0 / (1.0 + jnp.exp(-x))


def _silu(x):
    return x * _sigmoid(x)


def _log_sigmoid(x):
    return jnp.minimum(x, 0.0) - jnp.log1p(jnp.exp(-jnp.abs(x)))


def _head_mean_sq(y, g):
    hi, lo = _split2(y * y)
    return _dot(hi, g) + _dot(lo, g)


def _params(sem):
    return pltpu.CompilerParams(dimension_semantics=sem, vmem_limit_bytes=VMEM_LIMIT)


def _mod_kernel(s_ref, w_ref, b_ref, o_ref):
    s = _silu(s_ref[...])
    shi, slo = _split2(s)
    whi, wlo = _split2(w_ref[0])
    o_ref[0] = _dot(shi, whi) + _dot(slo, whi) + _dot(shi, wlo) + b_ref[0]


def _modulation(cvecs, w_mod, b_mod):
    L = w_mod.shape[0]
    tn = 1536
    return pl.pallas_call(
        _mod_kernel,
        grid=(L, 6 * D_MODEL // tn),
        in_specs=[pl.BlockSpec((16, D_MODEL), lambda l, j: (0, 0)),
                  pl.BlockSpec((1, D_MODEL, tn), lambda l, j: (l, 0, j)),
                  pl.BlockSpec((1, 1, tn), lambda l, j: (l, 0, j))],
        out_specs=pl.BlockSpec((1, 16, tn), lambda l, j: (l, 0, j)),
        out_shape=jax.ShapeDtypeStruct((L, 16, 6 * D_MODEL), F32),
        compiler_params=_params(("parallel", "parallel")),
        name="modulation",
    )(cvecs, w_mod, b_mod.reshape(L, 1, 6 * D_MODEL))


def _norm_mod(x, nw, shift, scale):
    ms = jnp.mean(x * x, axis=-1, keepdims=True)
    return x * lax.rsqrt(ms + EPS) * nw * (1.0 + scale) + shift


def _kin_kernel(x_ref, mod_ref, nw_ref, w_ref, wgc_ref, wgr_ref, gbc_ref, gbr_ref, cos_ref, sin_ref,
                qg_ref, kg_ref, g128_ref, ret_ref, ml_ref, gc_ref, gr_ref, na_ref, *, latent):
    h = _norm_mod(x_ref[...], nw_ref[...], mod_ref[0, 0:1, :], mod_ref[0, 1:2, :])
    hb = h.astype(BF16)
    for j in range(2 * RET_W // LANES):
        sl = slice(j * LANES, (j + 1) * LANES)
        y = _dot(hb, w_ref[:, sl])
        if latent:
            lane = lax.broadcasted_iota(jnp.int32, y.shape, 1)
            partner = jnp.where((lane % 32) < 16, pltpu.roll(y, LANES - 16, 1), pltpu.roll(y, 16, 1))
            y = y * cos_ref[...] + partner * sin_ref[...]
        ret_ref[:, sl] = y
    ret_ref[:, 2 * RET_W:] = _dot(hb, w_ref[:, 2 * RET_W:4 * RET_W])
    ml_ref[...] = _dot(hb, w_ref[:, 4 * RET_W:4 * RET_W + 4 * MLSTM_W])
    gc_ref[...] = _dot(hb, wgc_ref[...])[:, :16] + gbc_ref[...]
    gr_ref[...] = _dot_nt(wgr_ref[...], hb) + gbr_ref[...]
    base = 4 * RET_W + 4 * MLSTM_W
    g128 = g128_ref[...]
    for j in range(2 * NA_W // LANES):
        y = _dot(hb, w_ref[:, base + j * LANES: base + (j + 1) * LANES])
        gain = qg_ref[...] if j < NA_W // LANES else kg_ref[...]
        y = y * lax.rsqrt(_head_mean_sq(y, g128) + EPS) * gain
        na_ref[:, j * LANES:(j + 1) * LANES] = y.astype(BF16)
    na_ref[:, 2 * NA_W:] = _dot(hb, w_ref[:, base + 2 * NA_W:]).astype(BF16)


def _in_proj(x, mod, nw, w, wgc, wgr, gbc, gbr, cos, sin, qg, kg, g128, *, latent, seq):
    n = x.shape[0]
    tiles_per_seq = seq // TM
    if latent:
        mod_map = lambda i: (1 + i // tiles_per_seq, 0, 0)
        rope_map = lambda i: (i % tiles_per_seq, 0)
    else:
        mod_map = lambda i: (0, 0, 0)
        rope_map = lambda i: (0, 0)
    wtot = w.shape[1]
    const = lambda i: (0, 0)
    return pl.pallas_call(
        functools.partial(_kin_kernel, latent=latent),
        grid=(n // TM,),
        in_specs=[pl.BlockSpec((TM, D_MODEL), lambda i: (i, 0)),
                  pl.BlockSpec((1, 6, D_MODEL), mod_map),
                  pl.BlockSpec((1, D_MODEL), const),
                  pl.BlockSpec((D_MODEL, wtot), const),
                  pl.BlockSpec((D_MODEL, LANES), const),
                  pl.BlockSpec((16, D_MODEL), const),
                  pl.BlockSpec((1, 16), const),
                  pl.BlockSpec((16, 1), const),
                  pl.BlockSpec((TM, LANES), rope_map),
                  pl.BlockSpec((TM, LANES), rope_map),
                  pl.BlockSpec((1, LANES), const),
                  pl.BlockSpec((1, LANES), const),
                  pl.BlockSpec((LANES, LANES), const)],
        out_specs=[pl.BlockSpec((TM, 4 * RET_W), lambda i: (i, 0)),
                   pl.BlockSpec((TM, 4 * MLSTM_W), lambda i: (i, 0)),
                   pl.BlockSpec((TM, 16), lambda i: (i, 0)),
                   pl.BlockSpec((16, TM), lambda i: (0, i)),
                   pl.BlockSpec((TM, 3 * NA_W), lambda i: (i, 0))],
        out_shape=[jax.ShapeDtypeStruct((n, 4 * RET_W), F32),
                   jax.ShapeDtypeStruct((n, 4 * MLSTM_W), F32),
                   jax.ShapeDtypeStruct((n, 16), F32),
                   jax.ShapeDtypeStruct((16, n), F32),
                   jax.ShapeDtypeStruct((n, 3 * NA_W), BF16)],
        compiler_params=_params(("parallel",)),
        name="in_proj_lat" if latent else "in_proj_ctx",
    )(x, mod, nw, w, wgc, wgr, gbc, gbr, cos, sin, qg, kg, g128)


def _lane_head_masks(width):
    lane = lax.broadcasted_iota(jnp.int32, (1, width), 1)
    return [(lane // HEAD_DIM) == h for h in range(width // HEAD_DIM)]


def _block_diag_mask(width):
    r = lax.broadcasted_iota(jnp.int32, (width, width), 0) // HEAD_DIM
    c = lax.broadcasted_iota(jnp.int32, (width, width), 1) // HEAD_DIM
    return r == c


def _chunk_order(n_lat, d):
    return (lambda k: k) if d == 0 else (lambda k: n_lat - 1 - k)


def _ret_kernel(rc_ref, rl_ref, rdl_ref, rdh_ref, *out_refs, with_ctx):
    if with_ctx:
        oc_ref, ol_ref = out_refs
    else:
        (ol_ref,) = out_refs
        oc_ref = None
    n_ctx = rc_ref.shape[0] // CHUNK
    n_lat = rl_ref.shape[0] // CHUNK
    W = RET_W
    hm = _lane_head_masks(W)
    bd = _block_diag_mask(W)
    gmat = jnp.where(bd, 1.0 / HEAD_DIM, 0.0).astype(BF16)
    lgl = _log_sigmoid(rdl_ref[...])
    lgh = _log_sigmoid(rdh_ref[...])
    ii = lax.broadcasted_iota(jnp.int32, (CHUNK, CHUNK), 0)
    jj = lax.broadcasted_iota(jnp.int32, (CHUNK, CHUNK), 1)
    adiff = jnp.abs(ii - jj).astype(F32)
    idx = lax.broadcasted_iota(jnp.int32, (CHUNK, 1), 0).astype(F32)

    for d in range(2):
        pos = idx if d == 0 else (CHUNK - 1.0) - idx
        lg = lgl[d:d + 1, :]
        qdec = jnp.exp(lg * (pos + 1.0))
        kdec = jnp.exp(lg * ((CHUNK - 1.0) - pos))
        cdec = jnp.exp(lg * float(CHUNK))
        keep = (ii >= jj) if d == 0 else (jj >= ii)
        dmats = [jnp.where(keep, jnp.exp(lgh[d * RET_HEADS + h:d * RET_HEADS + h + 1, :] * adiff), 0.0)
                 for h in range(RET_HEADS)]

        def step(S, src_ref, dst_ref, r0, emit):
            blk = src_ref[pl.ds(r0, CHUNK), :]
            q = blk[:, 0:W]
            k = blk[:, W:2 * W] * (HEAD_DIM ** -0.5)
            vb = blk[:, 2 * W:3 * W].astype(BF16)
            if emit:
                kb = k.astype(BF16)
                y = _dot((q * qdec).astype(BF16), S.astype(BF16))
                for h in range(RET_HEADS):
                    att = _dot_nt(jnp.where(hm[h], q, 0.0).astype(BF16), kb) * dmats[h]
                    y = y + jnp.where(hm[h], _dot(att.astype(BF16), vb), 0.0)
                if d == 0:
                    dst_ref[pl.ds(r0, CHUNK), :] = y
                else:
                    y = y + dst_ref[pl.ds(r0, CHUNK), :]
                    g = blk[:, 3 * W:4 * W]
                    y = y * lax.rsqrt(_head_mean_sq(y, gmat) + EPS)
                    dst_ref[pl.ds(r0, CHUNK), :] = _silu(g) * y
            kd = (k * kdec).T.astype(BF16)
            return cdec * S + jnp.where(bd, _dot(kd, vb), 0.0)

        S = jnp.zeros((W, W), F32)
        for k in range(n_ctx):
            c = k if d == 0 else n_ctx - 1 - k
            S = step(S, rc_ref, oc_ref, c * CHUNK, with_ctx)
        order = _chunk_order(n_lat, d)

        def body(k, S):
            r0 = pl.multiple_of(order(k) * CHUNK, CHUNK)
            return step(S, rl_ref, ol_ref, r0, True)

        lax.fori_loop(0, n_lat, body, S)


def _retention(ret_c, ret_l, rdl, rdh, *, batch, with_ctx):
    lc = ret_c.shape[0] // batch
    ll = ret_l.shape[0] // batch
    out_specs = [pl.BlockSpec((ll, RET_W), lambda b: (b, 0))]
    out_shape = [jax.ShapeDtypeStruct((ret_l.shape[0], RET_W), F32)]
    if with_ctx:
        out_specs = [pl.BlockSpec((lc, RET_W), lambda b: (b, 0))] + out_specs
        out_shape = [jax.ShapeDtypeStruct((ret_c.shape[0], RET_W), F32)] + out_shape
    return pl.pallas_call(
        functools.partial(_ret_kernel, with_ctx=with_ctx),
        grid=(batch,),
        in_specs=[pl.BlockSpec((lc, 4 * RET_W), lambda b: (b, 0)),
                  pl.BlockSpec((ll, 4 * RET_W), lambda b: (b, 0)),
                  pl.BlockSpec((2, RET_W), lambda b: (0, 0)),
                  pl.BlockSpec((2 * RET_HEADS, LANES), lambda b: (0, 0))],
        out_specs=out_specs,
        out_shape=out_shape,
        compiler_params=_params(("parallel",)),
        name="retention",
    )(ret_c, ret_l, rdl, rdh)


def _ml_kernel(mc_ref, mlat_ref, gcc_ref, gcl_ref, grc_ref, grl_ref, cw_ref, cb_ref, *rest, with_ctx):
    if with_ctx:
        oc_ref, ol_ref, pad_ref, qkc_ref, qkl_ref = rest
    else:
        ol_ref, pad_ref, qkc_ref, qkl_ref = rest
        oc_ref = None
    n_ctx = mc_ref.shape[0] // CHUNK
    n_lat = mlat_ref.shape[0] // CHUNK
    W = MLSTM_W
    H = MLSTM_HEADS
    hm = _lane_head_masks(W)
    bd = _block_diag_mask(W)
    gmat = jnp.where(bd, 1.0 / HEAD_DIM, 0.0).astype(BF16)
    ones_bd = jnp.where(bd, 1.0, 0.0).astype(BF16)
    ii = lax.broadcasted_iota(jnp.int32, (CHUNK, CHUNK), 0)
    jj = lax.broadcasted_iota(jnp.int32, (CHUNK, CHUNK), 1)
    tri_lo = jnp.where(ii >= jj, 1.0, 0.0).astype(BF16)
    tri_up = jnp.where(jj >= ii, 1.0, 0.0).astype(BF16)

    w0, w1, w2 = cw_ref[0:1, :], cw_ref[1:2, :], cw_ref[2:3, :]
    cb = cb_ref[...]

    def conv_seq(src_ref, dst_ref, n_chunks):
        rows = n_chunks * CHUNK
        zero8 = jnp.zeros((8, 2 * W), F32)
        pad_ref[0:8, :] = zero8
        pad_ref[pl.ds(8 + rows, 8), :] = zero8

        def cp(k, _):
            r0 = pl.multiple_of(k * CHUNK, CHUNK)
            pad_ref[pl.ds(r0 + 8, CHUNK), :] = src_ref[pl.ds(r0, CHUNK), 0:2 * W]
            return 0

        lax.fori_loop(0, n_chunks, cp, 0)

        def cv(k, _):
            r0 = pl.multiple_of(k * CHUNK, CHUNK)
            xm = pad_ref[pl.ds(r0, CHUNK + 16), :]
            prev = pltpu.roll(xm, 1, 0)[8:8 + CHUNK]
            nxt = pltpu.roll(xm, CHUNK + 15, 0)[8:8 + CHUNK]
            y = w0 * prev + w1 * xm[8:8 + CHUNK] + w2 * nxt + cb
            dst_ref[pl.ds(r0, CHUNK), :] = _silu(y)
            return 0

        lax.fori_loop(0, n_chunks, cv, 0)

    conv_seq(mc_ref, qkc_ref, n_ctx)
    conv_seq(mlat_ref, qkl_ref, n_lat)

    for d in range(2):
        tri_c = tri_lo if d == 0 else tri_up
        tri_r = tri_up if d == 0 else tri_lo
        keep = (jj <= ii) if d == 0 else (jj >= ii)
        last = CHUNK - 1 if d == 0 else 0

        def step(carry, src_ref, qk_ref, gc_ref, gr_ref, dst_ref, r0, emit):
            C, n, ms = carry
            blk = src_ref[pl.ds(r0, CHUNK), :]
            qk = qk_ref[pl.ds(r0, CHUNK), :]
            q = qk[:, 0:W]
            k = qk[:, W:2 * W] * (HEAD_DIM ** -0.5)
            vb = blk[:, 2 * W:3 * W].astype(BF16)
            gcol = gc_ref[pl.ds(r0, CHUNK), :]
            grow = gr_ref[:, pl.ds(r0, CHUNK)]
            lf_c = _log_sigmoid(gcol)
            lf_r = _log_sigmoid(grow)
            c_hi, c_mid, c_lo = _split3(lf_c)
            b_col = _dot(tri_c, c_hi) + _dot(tri_c, c_mid) + _dot(tri_c, c_lo)
            r_hi, r_mid, r_lo = _split3(lf_r)
            b_row = _dot(r_hi, tri_r) + _dot(r_mid, tri_r) + _dot(r_lo, tri_r)
            b_last = b_col[last:last + 1, :]

            wk_l = jnp.zeros((CHUNK, W), F32)
            dp_l = jnp.zeros((1, W), F32)
            if emit:
                qb = q.astype(BF16)
                kb = k.astype(BF16)
                num = jnp.zeros((CHUNK, W), F32)
                den = jnp.zeros((CHUNK, W), F32)
                gi_l = jnp.zeros((CHUNK, W), F32)
                em_l = jnp.zeros((CHUNK, W), F32)
            new_ms = []
            for h in range(H):
                cf = 2 * H + d * H + h
                ci = d * H + h
                bc = b_col[:, cf:cf + 1]
                bl = b_last[:, cf:cf + 1]
                ic = gcol[:, ci:ci + 1]
                m_h = ms[h]
                ws = bl - bc + ic
                m_new = jnp.maximum(bl + m_h, jnp.max(ws, axis=0, keepdims=True))
                dprev = jnp.exp(bl + m_h - m_new)
                wk = jnp.exp(ws - m_new)
                wk_l = jnp.where(hm[h], wk, wk_l)
                dp_l = jnp.where(hm[h], dprev, dp_l)
                new_ms.append(m_new)
                if emit:
                    br = b_row[cf:cf + 1, :]
                    ir = grow[ci:ci + 1, :]
                    dm = jnp.where(keep, bc + (ir - br), -jnp.inf)
                    inter = bc + m_h
                    mrow = jnp.maximum(jnp.max(dm, axis=-1, keepdims=True), inter)
                    e = jnp.exp(dm - mrow)
                    gi = jnp.exp(inter - mrow)
                    s = _dot_nt(jnp.where(hm[h], qb, jnp.zeros_like(qb)), kb) * e
                    num = num + jnp.where(hm[h], _dot(s.astype(BF16), vb), 0.0)
                    den = jnp.where(hm[h], jnp.sum(s, axis=-1, keepdims=True), den)
                    gi_l = jnp.where(hm[h], gi, gi_l)
                    em_l = jnp.where(hm[h], jnp.exp(-mrow), em_l)
            if emit:
                qc = _dot(qb, C.astype(BF16))
                n_hi, n_lo = _split2(q * n)
                qn = _dot(n_hi, ones_bd) + _dot(n_lo, ones_bd)
                num = num + gi_l * qc
                den = den + gi_l * qn
                hout = num / jnp.maximum(jnp.abs(den), em_l)
                if d == 0:
                    dst_ref[pl.ds(r0, CHUNK), :] = hout
                else:
                    y = hout + dst_ref[pl.ds(r0, CHUNK), :]
                    o = blk[:, 3 * W:4 * W]
                    y = y * lax.rsqrt(_head_mean_sq(y, gmat) + EPS)
                    dst_ref[pl.ds(r0, CHUNK), :] = _sigmoid(o) * y
            kw = k * wk_l
            C = dp_l * C + jnp.where(bd, _dot(kw.T.astype(BF16), vb), 0.0)
            n = dp_l * n + jnp.sum(kw, axis=0, keepdims=True)
            return C, n, tuple(new_ms)

        carry = (jnp.zeros((W, W), F32), jnp.zeros((1, W), F32),
                 tuple(jnp.zeros((1, 1), F32) for _ in range(H)))
        for k in range(n_ctx):
            c = k if d == 0 else n_ctx - 1 - k
            carry = step(carry, mc_ref, qkc_ref, gcc_ref, grc_ref, oc_ref, c * CHUNK, with_ctx)
        order = _chunk_order(n_lat, d)

        def body(k, carry):
            r0 = pl.multiple_of(order(k) * CHUNK, CHUNK)
            return step(carry, mlat_ref, qkl_ref, gcl_ref, grl_ref, ol_ref, r0, True)

        lax.fori_loop(0, n_lat, body, carry)


def _mlstm(ml_c, ml_l, gc_c, gc_l, gr_c, gr_l, conv_w, conv_b, *, batch, with_ctx):
    lc = ml_c.shape[0] // batch
    ll = ml_l.shape[0] // batch
    out_specs = [pl.BlockSpec((ll, MLSTM_W), lambda b: (b, 0))]
    out_shape = [jax.ShapeDtypeStruct((ml_l.shape[0], MLSTM_W), F32)]
    if with_ctx:
        out_specs = [pl.BlockSpec((lc, MLSTM_W), lambda b: (b, 0))] + out_specs
        out_shape = [jax.ShapeDtypeStruct((ml_c.shape[0], MLSTM_W), F32)] + out_shape
    return pl.pallas_call(
        functools.partial(_ml_kernel, with_ctx=with_ctx),
        grid=(batch,),
        in_specs=[pl.BlockSpec((lc, 4 * MLSTM_W), lambda b: (b, 0)),
                  pl.BlockSpec((ll, 4 * MLSTM_W), lambda b: (b, 0)),
                  pl.BlockSpec((lc, 16), lambda b: (b, 0)),
                  pl.BlockSpec((ll, 16), lambda b: (b, 0)),
                  pl.BlockSpec((16, lc), lambda b: (0, b)),
                  pl.BlockSpec((16, ll), lambda b: (0, b)),
                  pl.BlockSpec((3, 2 * MLSTM_W), lambda b: (0, 0)),
                  pl.BlockSpec((1, 2 * MLSTM_W), lambda b: (0, 0))],
        out_specs=out_specs,
        out_shape=out_shape,
        scratch_shapes=[pltpu.VMEM((ll + 16, 2 * MLSTM_W), F32),
                        pltpu.VMEM((lc, 2 * MLSTM_W), F32),
                        pltpu.VMEM((ll, 2 * MLSTM_W), F32)],
        compiler_params=_params(("parallel",)),
        name="mlstm",
    )(ml_c, ml_l, gc_c, gc_l, gr_c, gr_l, conv_w, conv_b)


def _attn_pairs(q, k_tiles, v_tiles, bias_fn, o_ref):
    scale = HEAD_DIM ** -0.5
    n_pairs = q.shape[1] // LANES
    lane = lax.broadcasted_iota(jnp.int32, (1, LANES), 1)
    for p in range(n_pairs):
        sl = slice(p * LANES, (p + 1) * LANES)
        qp = q[:, sl]
        kp = [kt[:, sl] for kt in k_tiles]
        vp = [vt[:, sl] for vt in v_tiles]
        o_pair = jnp.zeros((q.shape[0], LANES), F32)
        for hh in range(2):
            mask = (lane < HEAD_DIM) if hh == 0 else (lane >= HEAD_DIM)
            qm = jnp.where(mask, qp, jnp.zeros_like(qp))
            s = []
            for i, kt in enumerate(kp):
                si = _dot_nt(qm, kt) * scale
                b = bias_fn(2 * p + hh, i)
                s.append(si if b is None else si + b)
            m = s[0].max(axis=-1, keepdims=True)
            for si in s[1:]:
                m = jnp.maximum(m, si.max(axis=-1, keepdims=True))
            l = jnp.zeros_like(m)
            o = jnp.zeros((q.shape[0], LANES), F32)
            for si, vt in zip(s, vp):
                pi = jnp.exp(si - m)
                l = l + pi.sum(axis=-1, keepdims=True)
                o = o + _dot(pi.astype(BF16), vt)
            o_pair = jnp.where(mask, o / l, o_pair)
        o_ref[:, sl] = o_pair.astype(o_ref.dtype)


def _na_kernel(q_ref, k0_ref, k1_ref, k2_ref, v0_ref, v1_ref, v2_ref, kc_ref, vc_ref, bias_ref, o_ref):
    def bias_fn(h, i):
        if i == 3:
            return None
        return bias_ref[0, h, :, i * NA_TQ:(i + 1) * NA_TQ]

    _attn_pairs(q_ref[...], [k0_ref[...], k1_ref[...], k2_ref[...], kc_ref[...]],
                [v0_ref[...], v1_ref[...], v2_ref[...], vc_ref[...]], bias_fn, o_ref)


def _neighbourhood_attention(na_l, na_c, bias, *, batch):
    n = na_l.shape[0]
    seq = n // batch
    nt = seq // NA_TQ
    band0 = lambda t: jnp.clip(t - 1, 0, nt - 3)
    cls = lambda t: jnp.where(t == 0, 0, jnp.where(t == nt - 1, 2, 1))
    blk = (NA_TQ, NA_W)

    def kv_spec(i, col):
        return pl.BlockSpec(blk, lambda t, b: (b * nt + band0(t) + i, col))

    return pl.pallas_call(
        _na_kernel,
        grid=(nt, batch),
        in_specs=[pl.BlockSpec(blk, lambda t, b: (b * nt + t, 0)),
                  kv_spec(0, 1), kv_spec(1, 1), kv_spec(2, 1),
                  kv_spec(0, 2), kv_spec(1, 2), kv_spec(2, 2),
                  pl.BlockSpec(blk, lambda t, b: (b, 1)),
                  pl.BlockSpec(blk, lambda t, b: (b, 2)),
                  pl.BlockSpec((1, NA_HEADS, NA_TQ, 3 * NA_TQ), lambda t, b: (cls(t), 0, 0, 0))],
        out_specs=pl.BlockSpec(blk, lambda t, b: (b * nt + t, 0)),
        out_shape=jax.ShapeDtypeStruct((n, NA_W), BF16),
        compiler_params=_params(("arbitrary", "arbitrary")),
        name="neighbourhood_attention",
    )(na_l, na_l, na_l, na_l, na_l, na_l, na_l, na_c, na_c, bias)


def _ctx_attn_kernel(q_ref, k_ref, v_ref, o_ref):
    _attn_pairs(q_ref[...], [k_ref[...]], [v_ref[...]], lambda h, i: None, o_ref)


def _context_attention(na_c, *, batch):
    n = na_c.shape[0]
    lc = n // batch
    blk = (lc, NA_W)
    return pl.pallas_call(
        _ctx_attn_kernel,
        grid=(batch,),
        in_specs=[pl.BlockSpec(blk, lambda b: (b, 0)),
                  pl.BlockSpec(blk, lambda b: (b, 1)),
                  pl.BlockSpec(blk, lambda b: (b, 2))],
        out_specs=pl.BlockSpec(blk, lambda b: (b, 0)),
        out_shape=jax.ShapeDtypeStruct((n, NA_W), BF16),
        compiler_params=_params(("parallel",)),
        name="context_attention",
    )(na_c, na_c, na_c)


def _na_bias_tables(rpb, seq):
    rows = seq // GRID_W
    rq = NA_TQ // GRID_W
    H = rpb.shape[0]
    n_dc = 2 * NA_WIN_C - 1
    col = np.arange(GRID_W)
    sc = np.clip(col - NA_WIN_C // 2, 0, GRID_W - NA_WIN_C)
    col_ok = (col[None, :] >= sc[:, None]) & (col[None, :] < sc[:, None] + NA_WIN_C)
    dc = np.clip(col[None, :] - col[:, None] + NA_WIN_C - 1, 0, n_dc - 1)
    pick = (dc[None] == np.arange(n_dc)[:, None, None]).astype(np.float32).reshape(n_dc, GRID_W * GRID_W)
    blocks = jnp.einsum('hrc,cx->hrx', rpb.astype(F32), jnp.asarray(pick), precision=lax.Precision.HIGHEST)
    blocks = jnp.where(col_ok[None, None], blocks.reshape(H, -1, GRID_W, GRID_W), NEG)
    masked = jnp.full((H, GRID_W, GRID_W), NEG, F32)
    tabs = []
    for r0, bs in ((0, 0), (2 * rq, rq), (rows - rq, rows - 3 * rq)):
        per_q = []
        for qr in range(r0, r0 + rq):
            sr = min(max(qr - NA_WIN_R // 2, 0), rows - NA_WIN_R)
            per_k = [blocks[:, kr - qr + NA_WIN_R - 1] if sr <= kr < sr + NA_WIN_R else masked
                     for kr in range(bs, bs + 3 * rq)]
            per_q.append(jnp.stack(per_k, axis=2))
        tabs.append(jnp.stack(per_q, axis=1).reshape(H, NA_TQ, 3 * NA_TQ))
    return jnp.stack(tabs)


def _out_kernel(x_ref, mod_ref, r_ref, m_ref, a_ref, w_ref, o_ref):
    cat = jnp.concatenate([r_ref[...].astype(BF16), m_ref[...].astype(BF16), a_ref[...]], axis=-1)
    o_ref[...] = x_ref[...] + mod_ref[0, 2:3, :] * _dot(cat, w_ref[...])


def _out_proj(x, mod, ret_y, ml_y, na_y, w, *, latent, seq):
    n = x.shape[0]
    tiles_per_seq = seq // TM
    mod_map = (lambda i: (1 + i // tiles_per_seq, 0, 0)) if latent else (lambda i: (0, 0, 0))
    return pl.pallas_call(
        _out_kernel,
        grid=(n // TM,),
        in_specs=[pl.BlockSpec((TM, D_MODEL), lambda i: (i, 0)),
                  pl.BlockSpec((1, 6, D_MODEL), mod_map),
                  pl.BlockSpec((TM, RET_W), lambda i: (i, 0)),
                  pl.BlockSpec((TM, MLSTM_W), lambda i: (i, 0)),
                  pl.BlockSpec((TM, NA_W), lambda i: (i, 0)),
                  pl.BlockSpec((D_MODEL, D_MODEL), lambda i: (0, 0))],
        out_specs=pl.BlockSpec((TM, D_MODEL), lambda i: (i, 0)),
        out_shape=jax.ShapeDtypeStruct((n, D_MODEL), F32),
        compiler_params=_params(("parallel",)),
        name="out_proj",
    )(x, mod, ret_y, ml_y, na_y, w)


def _ffn_kernel(x_ref, mod_ref, nw_ref, wg_ref, wu_ref, wd_ref, o_ref, h_ref):
    f = pl.program_id(1)

    @pl.when(f == 0)
    def _():
        h = _norm_mod(x_ref[...], nw_ref[...], mod_ref[0, 3:4, :], mod_ref[0, 4:5, :])
        h_ref[...] = h.astype(BF16)

    hb = h_ref[...]
    a = _dot(hb, wg_ref[...])
    u = _dot(hb, wu_ref[...])
    y = _dot((_silu(a) * u).astype(BF16), wd_ref[...])

    @pl.when(f == 0)
    def _():
        o_ref[...] = y

    @pl.when(f > 0)
    def _():
        o_ref[...] += y

    @pl.when(f == pl.num_programs(1) - 1)
    def _():
        o_ref[...] = x_ref[...] + mod_ref[0, 5:6, :] * o_ref[...]


def _dense_ffn(x, mod, nw, wg, wu, wd, *, latent, seq):
    n = x.shape[0]
    tiles_per_seq = seq // TM
    mod_map = (lambda i, f: (1 + i // tiles_per_seq, 0, 0)) if latent else (lambda i, f: (0, 0, 0))
    return pl.pallas_call(
        _ffn_kernel,
        grid=(n // TM, D_FF // TF),
        in_specs=[pl.BlockSpec((TM, D_MODEL), lambda i, f: (i, 0)),
                  pl.BlockSpec((1, 6, D_MODEL), mod_map),
                  pl.BlockSpec((1, D_MODEL), lambda i, f: (0, 0)),
                  pl.BlockSpec((D_MODEL, TF), lambda i, f: (0, f)),
                  pl.BlockSpec((D_MODEL, TF), lambda i, f: (0, f)),
                  pl.BlockSpec((TF, D_MODEL), lambda i, f: (f, 0))],
        out_specs=pl.BlockSpec((TM, D_MODEL), lambda i, f: (i, 0)),
        out_shape=jax.ShapeDtypeStruct((n, D_MODEL), F32),
        scratch_shapes=[pltpu.VMEM((TM, D_MODEL), BF16)],
        compiler_params=_params(("parallel", "arbitrary")),
        name="dense_ffn",
    )(x, mod, nw, wg, wu, wd)


def _moe_kernel(x_ref, mod_ref, nw_ref, wr_ref, wg_ref, wu_ref, wd_ref, o_ref,
                h_ref, sel_ref, pos_ref, cmb_ref, xe_ref, y_ref, tri_ref):
    i, e, f = pl.program_id(0), pl.program_id(1), pl.program_id(2)
    T, R = MOE_T, MOE_R
    last_e = pl.num_programs(1) - 1
    last_f = pl.num_programs(2) - 1

    @pl.when((i == 0) & (e == 0) & (f == 0))
    def _():
        a = lax.broadcasted_iota(jnp.int32, (T, T), 0)
        b = lax.broadcasted_iota(jnp.int32, (T, T), 1)
        tri_ref[...] = jnp.where(a < b, 1.0, 0.0).astype(BF16)

    @pl.when((e == 0) & (f == 0))
    def _():
        h = _norm_mod(x_ref[...], nw_ref[...], mod_ref[0, 3:4, :], mod_ref[0, 4:5, :])
        h_ref[...] = h.astype(BF16)
        hhi, hlo = _split2(h)
        whi, wlo = _split2(wr_ref[...])
        logit = _dot_nt(whi, hhi) + _dot_nt(whi, hlo) + _dot_nt(wlo, hhi)
        eid = lax.broadcasted_iota(jnp.int32, (E_PAD, T), 0)
        logit = jnp.where(eid < N_EXPERTS, logit, -jnp.inf)
        m1 = jnp.max(logit, axis=0, keepdims=True)
        i1 = jnp.min(jnp.where(logit == m1, eid, E_PAD), axis=0, keepdims=True)
        rest = jnp.where(eid == i1, -jnp.inf, logit)
        m2 = jnp.max(rest, axis=0, keepdims=True)
        i2 = jnp.min(jnp.where(rest == m2, eid, E_PAD), axis=0, keepdims=True)
        e2 = jnp.exp(m2 - m1)
        w1 = 1.0 / (1.0 + e2)
        w2 = e2 / (1.0 + e2)
        sel = jnp.where((eid == i1) | (eid == i2), 1.0, 0.0)
        cmb = jnp.where(eid == i1, w1, jnp.where(eid == i2, w2, 0.0))
        pos = _dot(sel.astype(BF16), tri_ref[...])
        for k in range(N_EXPERTS):
            sel_ref[k] = sel[k:k + 1, :]
            pos_ref[k] = pos[k:k + 1, :]
            cmb_ref[k] = cmb[k:k + 1, :]
        o_ref[...] = jnp.zeros_like(o_ref)

    sel_row = sel_ref[e]
    pos_row = pos_ref[e]
    cmb_row = cmb_ref[e]
    count = jnp.sum(sel_row).astype(jnp.int32)
    n_chunks = (count + (R - 1)) // R

    def onehot(r):
        slot = (lax.broadcasted_iota(jnp.int32, (R, T), 0) + r * R).astype(F32)
        return jnp.where((pos_row == slot) & (sel_row > 0.0), 1.0, 0.0)

    def chunk(r, _):
        rows = pl.ds(pl.multiple_of(r * R, R), R)

        @pl.when(f == 0)
        def _():
            xe_ref[rows, :] = _dot(onehot(r).astype(BF16), h_ref[...]).astype(BF16)

        xr = xe_ref[rows, :]
        a = _dot(xr, wg_ref[0])
        u = _dot(xr, wu_ref[0])
        y = _dot((_silu(a) * u).astype(BF16), wd_ref[0])

        @pl.when(f == 0)
        def _():
            y_ref[rows, :] = y

        @pl.when(f > 0)
        def _():
            y_ref[rows, :] += y

        @pl.when(f == last_f)
        def _():
            p = onehot(r)
            w = jnp.sum(p * cmb_row, axis=-1, keepdims=True)
            yhi, ylo = _split2(y_ref[rows, :] * w)
            pt = p.T.astype(BF16)
            o_ref[...] += _dot(pt, yhi) + _dot(pt, ylo)

        return 0

    lax.fori_loop(0, n_chunks, chunk, 0)

    @pl.when((e == last_e) & (f == last_f))
    def _():
        o_ref[...] = x_ref[...] + mod_ref[0, 5:6, :] * o_ref[...]


def _moe_ffn(x, mod, nw, wr, wg, wu, wd, *, seq):
    n = x.shape[0]
    T = MOE_T
    tiles_per_seq = seq // T
    return pl.pallas_call(
        _moe_kernel,
        grid=(n // T, N_EXPERTS, D_FF // TF),
        in_specs=[pl.BlockSpec((T, D_MODEL), lambda i, e, f: (i, 0)),
                  pl.BlockSpec((1, 6, D_MODEL), lambda i, e, f: (1 + i // tiles_per_seq, 0, 0)),
                  pl.BlockSpec((1, D_MODEL), lambda i, e, f: (0, 0)),
                  pl.BlockSpec((E_PAD, D_MODEL), lambda i, e, f: (0, 0)),
                  pl.BlockSpec((1, D_MODEL, TF), lambda i, e, f: (e, 0, f)),
                  pl.BlockSpec((1, D_MODEL, TF), lambda i, e, f: (e, 0, f)),
                  pl.BlockSpec((1, TF, D_MODEL), lambda i, e, f: (e, f, 0))],
        out_specs=pl.BlockSpec((T, D_MODEL), lambda i, e, f: (i, 0)),
        out_shape=jax.ShapeDtypeStruct((n, D_MODEL), F32),
        scratch_shapes=[pltpu.VMEM((T, D_MODEL), BF16),
                        pltpu.VMEM((N_EXPERTS, 1, T), F32),
                        pltpu.VMEM((N_EXPERTS, 1, T), F32),
                        pltpu.VMEM((N_EXPERTS, 1, T), F32),
                        pltpu.VMEM((T, D_MODEL), BF16),
                        pltpu.VMEM((T, D_MODEL), F32),
                        pltpu.VMEM((T, T), BF16)],
        compiler_params=_params(("arbitrary", "arbitrary", "arbitrary")),
        name="moe_ffn",
    )(x, mod, nw, wr, wg, wu, wd)


def _rope_tables(seq):
    t = np.arange(seq)
    half = HEAD_DIM // 2
    inv = ROPE_BASE ** (-np.arange(0, half, 2, dtype=np.float32) / half)
    ang_r = (t // GRID_W).astype(np.float32)[:, None] * inv[None, :]
    ang_c = (t % GRID_W).astype(np.float32)[:, None] * inv[None, :]
    ang = jnp.asarray(np.concatenate([ang_r, ang_r, ang_c, ang_c], axis=1), F32)
    sign = np.concatenate([-np.ones(16), np.ones(16), -np.ones(16), np.ones(16)]).astype(np.float32)
    cos = jnp.cos(ang)
    sin = jnp.sin(ang) * sign[None, :]
    return jnp.tile(cos, (1, LANES // HEAD_DIM)), jnp.tile(sin, (1, LANES // HEAD_DIM))


def _split_w_in(w):
    a = 4 * RET_W + 4 * MLSTM_W
    main = jnp.concatenate([w[:, :a], w[:, a + 16:]], axis=1).astype(BF16)
    gates = w[:, a:a + 16]
    wgc = jnp.pad(gates, ((0, 0), (0, LANES - 16))).astype(BF16)
    wgr = gates.T.astype(BF16)
    return main, wgc, wgr


def kernel(x, c, ctx, c_ctx, norm_mix, norm_ffn, w_mod, b_mod, w_in, ret_decay, mlstm_conv_w, mlstm_conv_b,
           mlstm_gate_b, na_q_gain, na_k_gain, na_rpb, w_out, ffn_w_gate, ffn_w_up, ffn_w_down, moe_router,
           moe_w_gate, moe_w_up, moe_w_down):
    B, seq, D = x.shape
    lctx = ctx.shape[1]
    depth = w_in.shape[0]
    assert D == D_MODEL and B + 1 <= 16 and seq % TM == 0 and lctx % CHUNK == 0 and lctx == NA_TQ

    cvecs = jnp.zeros((16, D), F32).at[0].set(c_ctx).at[1:B + 1].set(c)
    mod = _modulation(cvecs, w_mod, b_mod).reshape(depth, 16, 6, D)

    cos, sin = _rope_tables(seq)
    g128 = jnp.asarray(np.kron(np.eye(LANES // HEAD_DIM), np.full((HEAD_DIM, HEAD_DIM), 1.0 / HEAD_DIM)), BF16)

    xl = x.reshape(B * seq, D)
    xc = ctx.reshape(B * lctx, D)
    for l in range(depth):
        need_ctx = l < depth - 1
        w_main, wgc, wgr = _split_w_in(w_in[l])
        gbc = mlstm_gate_b[l].reshape(1, 16).astype(F32)
        gbr = gbc.reshape(16, 1)
        qg = jnp.tile(na_q_gain[l].astype(F32), LANES // HEAD_DIM)[None, :]
        kg = jnp.tile(na_k_gain[l].astype(F32), LANES // HEAD_DIM)[None, :]
        nw_mix = norm_mix[l].astype(F32)[None, :]
        nw_ffn = norm_ffn[l].astype(F32)[None, :]
        prep = functools.partial(_in_proj, mod=mod[l], nw=nw_mix, w=w_main, wgc=wgc, wgr=wgr, gbc=gbc, gbr=gbr,
                                 cos=cos, sin=sin, qg=qg, kg=kg, g128=g128, seq=seq)
        ret_l, ml_l, gc_l, gr_l, na_l = prep(xl, latent=True)
        ret_c, ml_c, gc_c, gr_c, na_c = prep(xc, latent=False)

        rd = ret_decay[l].astype(F32)
        rdl = jnp.repeat(rd, HEAD_DIM, axis=1)
        rdh = jnp.broadcast_to(rd.reshape(2 * RET_HEADS, 1), (2 * RET_HEADS, LANES))
        ret_y = _retention(ret_c, ret_l, rdl, rdh, batch=B, with_ctx=need_ctx)
        ml_y = _mlstm(ml_c, ml_l, gc_c, gc_l, gr_c, gr_l, mlstm_conv_w[l].astype(F32),
                      mlstm_conv_b[l].astype(F32)[None, :], batch=B, with_ctx=need_ctx)
        bias = _na_bias_tables(na_rpb[l], seq)
        na_y = _neighbourhood_attention(na_l, na_c, bias, batch=B)
        wo = w_out[l].astype(BF16)
        if need_ctx:
            na_yc = _context_attention(na_c, batch=B)
            xc = _out_proj(xc, mod[l], ret_y[0], ml_y[0], na_yc, wo, latent=False, seq=seq)
        xl = _out_proj(xl, mod[l], ret_y[-1], ml_y[-1], na_y, wo, latent=True, seq=seq)

        j = l // 2
        if l % 2 == 0:
            wg, wu, wd = ffn_w_gate[j].astype(BF16), ffn_w_up[j].astype(BF16), ffn_w_down[j].astype(BF16)
            xl = _dense_ffn(xl, mod[l], nw_ffn, wg, wu, wd, latent=True, seq=seq)
            if need_ctx:
                xc = _dense_ffn(xc, mod[l], nw_ffn, wg, wu, wd, latent=False, seq=seq)
        else:
            if need_ctx:
                raise NotImplementedError("context stream through a MoE layer")
            wr = jnp.pad(moe_router[j].astype(F32).T, ((0, E_PAD - N_EXPERTS), (0, 0)))
            xl = _moe_ffn(xl, mod[l], nw_ffn, wr, moe_w_gate[j].astype(BF16), moe_w_up[j].astype(BF16),
                          moe_w_down[j].astype(BF16), seq=seq)
    return xl.reshape(B, seq, D)
```
